```python
import math
import jax, jax.numpy as jnp
from jax import lax
import numpy as np

D_MODEL = 1024
BATCH = 8
SEQ = 2048
DEPTH = 4
DEC_BATCH = 128
DEC_SEQ = 8
PAST_LEN = 2048
PAGE_SIZE = 128

POOL_WIDTH = D_MODEL // 4
POOL_WINDOWS = (2, 4, 8, 16)
N_POOL_GROUPS = len(POOL_WINDOWS)
POOL_GROUP = POOL_WIDTH // N_POOL_GROUPS
POOL_STATE = max(POOL_WINDOWS) - 1
HEAD_DIM = 64
DIFF_HEADS = (D_MODEL // 2) // (2 * HEAD_DIM)
STICK_HEADS = (D_MODEL // 4) // HEAD_DIM
ROT_DIM = HEAD_DIM // 4
ROPE_THETA = 500000.0
Q_BLOCK = 128
DIFF_QK = DIFF_HEADS * 2 * HEAD_DIM
DIFF_V = DIFF_HEADS * 2 * HEAD_DIM
STICK_W = STICK_HEADS * HEAD_DIM
IN_SIZES = (POOL_WIDTH, DIFF_QK, DIFF_QK, DIFF_V, STICK_W, STICK_W, STICK_W, D_MODEL, D_MODEL, D_MODEL)
IN_WIDTH = sum(IN_SIZES)
IN_OFFSETS = tuple(int(v) for v in np.cumsum(IN_SIZES)[:-1])
N_GROUPS = 4
EXPERTS_PER_GROUP = 4
N_EXPERTS = N_GROUPS * EXPERTS_PER_GROUP
TOP_K = 2
D_EXPERT = D_MODEL // 4
EPS = 1e-6

kernel_name = 'hybrid_pool_diff_stick_hmoe_step'


def rms_norm(x, g):
    x32 = x.astype(jnp.float32)
    y = x32 * lax.rsqrt(jnp.mean(x32 * x32, axis=-1, keepdims=True) + EPS)
    return (y * g.astype(jnp.float32)).astype(x.dtype)


def partial_rope(x, positions):
    half = ROT_DIM // 2
    inv = ROPE_THETA ** (-jnp.arange(0, ROT_DIM, 2, dtype=jnp.float32) / ROT_DIM)
    ang = positions.astype(jnp.float32)[:, None] * inv[None, :]
    cos = jnp.cos(ang)[None, :, None, None, :]
    sin = jnp.sin(ang)[None, :, None, None, :]
    x32 = x.astype(jnp.float32)
    x1 = x32[..., :half]
    x2 = x32[..., half:ROT_DIM]
    out = jnp.concatenate([x1 * cos - x2 * sin, x2 * cos + x1 * sin, x32[..., ROT_DIM:]], axis=-1)
    return out.astype(x.dtype)


def query_blocks(fn, q, q_pos):
    b, t = q.shape[:2]
    qb = min(Q_BLOCK, t)
    nb = t // qb
    qs = jnp.swapaxes(q.reshape((b, nb, qb) + q.shape[2:]), 0, 1)
    out = lax.map(lambda args: fn(args[0], args[1]), (qs, q_pos.reshape(nb, qb)))
    return jnp.swapaxes(out, 0, 1).reshape((b, t) + out.shape[3:])


def pool_mixer(a_in, prefix, positions, pool_w, pool_scale):
    t = a_in.shape[1]
    u = jnp.concatenate([prefix.astype(a_in.dtype), a_in], axis=1)
    cs = jnp.cumsum(u.astype(jnp.float32), axis=1)
    cs = jnp.concatenate([jnp.zeros_like(cs[:, :1]), cs], axis=1)
    a32 = a_in.astype(jnp.float32)
    groups = []
    for g, w in enumerate(POOL_WINDOWS):
        sl = slice(g * POOL_GROUP, (g + 1) * POOL_GROUP)
        s = cs[:, POOL_STATE + 1:POOL_STATE + 1 + t, sl] - cs[:, POOL_STATE + 1 - w:POOL_STATE + 1 - w + t, sl]
        cnt = jnp.minimum(w, positions + 1).astype(jnp.float32)[None, :, None]
        groups.append(s / cnt - a32[..., sl])
    pooled = jnp.stack(groups, axis=2).astype(a_in.dtype)
    mixed = jnp.einsum('btgc,gce->btge', pooled, pool_w)
    return mixed.reshape(a_in.shape) * pool_scale, u[:, -POOL_STATE:]


def diff_attention(q, k, v, q_pos, lam, lam_init, subln_g):
    k32 = k.astype(jnp.float32)
    k_pos = jnp.arange(k.shape[1])
    scale = HEAD_DIM ** -0.5

    def block(qb, pb):
        s = jnp.einsum('bqhcd,bkhcd->bhcqk', qb.astype(jnp.float32), k32) * scale
        mask = k_pos[None, :] <= pb[:, None]
        p = jax.nn.softmax(jnp.where(mask, s, -jnp.inf), axis=-1)
        w = p[:, :, 0] - lam * p[:, :, 1]
        return jnp.einsum('bhqk,bkhe->bqhe', w.astype(v.dtype), v)

    o = query_blocks(block, q, q_pos)
    return rms_norm(o, subln_g) * (1.0 - lam_init)


def stick_breaking(q, k, v, q_pos):
    k32 = k.astype(jnp.float32)
    k_pos = jnp.arange(k.shape[1])
    scale = HEAD_DIM ** -0.5

    def block(qb, pb):
        z = jnp.einsum('bqhd,bkhd->bhqk', qb.astype(jnp.float32), k32) * scale
        mask = k_pos[None, :] < pb[:, None]
        log_keep = jnp.where(mask, jax.nn.log_sigmoid(-z), 0.0)
        later = lax.cumsum(log_keep, axis=3, reverse=True) - log_keep
        a = jnp.where(mask, jnp.exp(jax.nn.log_sigmoid(z) + later), 0.0)
        return jnp.einsum('bhqk,bkhd->bqhd', a.astype(v.dtype), v)

    return query_blocks(block, q, q_pos)


def hier_moe(h, w_group, b_group, w_router, b_router, w_e_gate, w_e_up, w_e_down):
    shp = h.shape
    hf = h.reshape(-1, shp[-1])
    n = hf.shape[0]
    g_logits = (hf @ w_group).astype(jnp.float32) + b_group.astype(jnp.float32)
    g_idx = jnp.argmax(g_logits, axis=-1)
    g_prob = jnp.take_along_axis(jax.nn.softmax(g_logits, axis=-1), g_idx[:, None], axis=1)
    e_logits = ((hf @ w_router).astype(jnp.float32) + b_router.astype(jnp.float32)).reshape(n, N_GROUPS, EXPERTS_PER_GROUP)
    e_in = e_logits[jnp.arange(n), g_idx]
    top_v, top_i = lax.top_k(e_in, TOP_K)
    gate = jax.nn.softmax(top_v, axis=-1) * g_prob
    ids = g_idx[:, None] * EXPERTS_PER_GROUP + top_i
    combine = jnp.sum(jax.nn.one_hot(ids, N_EXPERTS, dtype=jnp.float32) * gate[..., None], axis=1)
    a = jnp.einsum('nd,edf->nef', hf, w_e_gate)
    u = jnp.einsum('nd,edf->nef', hf, w_e_up)
    act = jax.nn.silu(a) * u * combine[:, :, None].astype(h.dtype)
    out = jnp.einsum('nef,efd->nd', act, w_e_down)
    return out.reshape(shp)


def layer(x, positions, pool_prefix, past_kb, past_vb, past_kc, past_vc, lp, lam_init):
    (norm1_g, w_in, pool_w, pool_scale, qn_g, kn_g, lam_q1, lam_k1, lam_q2, lam_k2, subln_g,
     w_proj_a, w_proj_b, w_proj_c, w_out, norm2_g, w_group, b_group, w_router, b_router,
     w_e_gate, w_e_up, w_e_down) = lp
    b, t, _ = x.shape
    h = rms_norm(x, norm1_g)
    a_in, qb_, kb_, vb_, qc, kc, vc, ga, gb, gc = jnp.split(h @ w_in, IN_OFFSETS, axis=-1)
    o_a, new_pool = pool_mixer(a_in, pool_prefix, positions, pool_w, pool_scale)
    qb_ = partial_rope(rms_norm(qb_.reshape(b, t, DIFF_HEADS, 2, HEAD_DIM), qn_g), positions)
    kb_ = partial_rope(rms_norm(kb_.reshape(b, t, DIFF_HEADS, 2, HEAD_DIM), kn_g), positions)
    kb_rows = kb_.reshape(b, t, DIFF_HEADS, 2 * HEAD_DIM)
    vb_ = vb_.reshape(b, t, DIFF_HEADS, 2 * HEAD_DIM)
    f32 = jnp.float32
    lam = (jnp.exp(jnp.sum(lam_q1.astype(f32) * lam_k1.astype(f32)))
           - jnp.exp(jnp.sum(lam_q2.astype(f32) * lam_k2.astype(f32))) + lam_init)
    k_all = jnp.concatenate([past_kb.astype(kb_rows.dtype), kb_rows], axis=1)
    k_all = k_all.reshape(b, k_all.shape[1], DIFF_HEADS, 2, HEAD_DIM)
    v_all = jnp.concatenate([past_vb.astype(vb_.dtype), vb_], axis=1)
    o_b = diff_attention(qb_, k_all, v_all, positions, lam, lam_init, subln_g)
    qc = qc.reshape(b, t, STICK_HEADS, HEAD_DIM)
    kc = kc.reshape(b, t, STICK_HEADS, HEAD_DIM)
    vc = vc.reshape(b, t, STICK_HEADS, HEAD_DIM)
    o_c = stick_breaking(qc, jnp.concatenate([past_kc.astype(kc.dtype), kc], axis=1),
                         jnp.concatenate([past_vc.astype(vc.dtype), vc], axis=1), positions)
    merged = (jax.nn.sigmoid(ga) * (o_a @ w_proj_a)
              + jax.nn.sigmoid(gb) * (o_b.reshape(b, t, DIFF_V) @ w_proj_b)
              + jax.nn.sigmoid(gc) * (o_c.reshape(b, t, STICK_W) @ w_proj_c))
    x = x + merged @ w_out
    x = x + hier_moe(rms_norm(x, norm2_g), w_group, b_group, w_router, b_router, w_e_gate, w_e_up, w_e_down)
    return x, (kb_rows, vb_, kc, vc, new_pool)


def setup_inputs(seed: int = 0) -> dict:
    key = jax.random.key(seed)
    keys = jax.random.split(key, 40)
    counter = iter(range(40))

    def nrm(shape, scale=1.0):
        return scale * jax.random.normal(keys[next(counter)], shape, jnp.float32)

    def gain(shape):
        return 1.0 + nrm(shape, 0.05)

    n_pages = PAST_LEN // PAGE_SIZE
    n_pool_pages = (DEC_BATCH * n_pages * 5) // 4
    page_table = jax.random.permutation(keys[next(counter)], n_pool_pages)[:DEC_BATCH * n_pages]
    page_table = page_table.reshape(DEC_BATCH, n_pages).astype(jnp.int32)
    L = DEPTH
    D = D_MODEL
    return {
        'x_prompt': nrm((BATCH, SEQ, D)),
        'x_sample': nrm((DEC_BATCH, DEC_SEQ, D)),
        'cache_kb': nrm((L, n_pool_pages, PAGE_SIZE, DIFF_HEADS, 2 * HEAD_DIM)),
        'cache_vb': nrm((L, n_pool_pages, PAGE_SIZE, DIFF_HEADS, 2 * HEAD_DIM)),
        'cache_kc': nrm((L, n_pool_pages, PAGE_SIZE, STICK_HEADS, HEAD_DIM)),
        'cache_vc': nrm((L, n_pool_pages, PAGE_SIZE, STICK_HEADS, HEAD_DIM)),
        'state_pool': nrm((L, DEC_BATCH, POOL_STATE, POOL_WIDTH)),
        'page_table': page_table,
        'norm1_g': gain((L, D)),
        'w_in': nrm((L, D, IN_WIDTH), D ** -0.5),
        'pool_w': nrm((L, N_POOL_GROUPS, POOL_GROUP, POOL_GROUP), POOL_GROUP ** -0.5),
        'pool_scale': gain((L, POOL_WIDTH)),
        'qn_g': gain((L, 2, HEAD_DIM)),
        'kn_g': gain((L, 2, HEAD_DIM)),
        'lam_q1': nrm((L, HEAD_DIM), 0.1),
        'lam_k1': nrm((L, HEAD_DIM), 0.1),
        'lam_q2': nrm((L, HEAD_DIM), 0.1),
        'lam_k2': nrm((L, HEAD_DIM), 0.1),
        'subln_g': gain((L, 2 * HEAD_DIM)),
        'w_proj_a': nrm((L, POOL_WIDTH, D), POOL_WIDTH ** -0.5),
        'w_proj_b': nrm((L, DIFF_V, D), DIFF_V ** -0.5),
        'w_proj_c': nrm((L, STICK_W, D), STICK_W ** -0.5),
        'w_out': nrm((L, D, D), D ** -0.5),
        'norm2_g': gain((L, D)),
        'w_group': nrm((L, D, N_GROUPS), D ** -0.5),
        'b_group': nrm((L, N_GROUPS), 0.01),
        'w_router': nrm((L, D, N_EXPERTS), D ** -0.5),
        'b_router': nrm((L, N_EXPERTS), 0.01),
        'w_e_gate': nrm((L, N_EXPERTS, D, D_EXPERT), D ** -0.5),
        'w_e_up': nrm((L, N_EXPERTS, D, D_EXPERT), D ** -0.5),
        'w_e_down': nrm((L, N_EXPERTS, D_EXPERT, D), D_EXPERT ** -0.5),
    }


def reference(x_prompt, x_sample, cache_kb, cache_vb, cache_kc, cache_vc, state_pool, page_table,
              norm1_g, w_in, pool_w, pool_scale, qn_g, kn_g, lam_q1, lam_k1, lam_q2, lam_k2, subln_g,
              w_proj_a, w_proj_b, w_proj_c, w_out, norm2_g, w_group, b_group, w_router, b_router,
              w_e_gate, w_e_up, w_e_down):
    bp, tp = x_prompt.shape[0], x_prompt.shape[1]
    bs, ts = x_sample.shape[0], x_sample.shape[1]
    past_len = page_table.shape[1] * cache_kb.shape[2]
    pos_p = jnp.arange(tp, dtype=jnp.int32)
    pos_s = past_len + jnp.arange(ts, dtype=jnp.int32)

    def gather(cache):
        return cache[page_table].reshape((bs, past_len) + cache.shape[2:])

    dt = x_prompt.dtype
    empty_kb = jnp.zeros((bp, 0, DIFF_HEADS, 2 * HEAD_DIM), dt)
    empty_kc = jnp.zeros((bp, 0, STICK_HEADS, HEAD_DIM), dt)
    zero_pool = jnp.zeros((bp, POOL_STATE, POOL_WIDTH), dt)
    yp, ys = x_prompt, x_sample
    rows_p, rows_s = [], []
    for l in range(DEPTH):
        lp = tuple(p[l] for p in (norm1_g, w_in, pool_w, pool_scale, qn_g, kn_g, lam_q1, lam_k1, lam_q2, lam_k2,
                                  subln_g, w_proj_a, w_proj_b, w_proj_c, w_out, norm2_g, w_group, b_group,
                                  w_router, b_router, w_e_gate, w_e_up, w_e_down))
        lam_init = 0.8 - 0.6 * math.exp(-0.3 * l)
        yp, r_p = layer(yp, pos_p, zero_pool, empty_kb, empty_kb, empty_kc, empty_kc, lp, lam_init)
        ys, r_s = layer(ys, pos_s, state_pool[l], gather(cache_kb[l]), gather(cache_vb[l]),
                        gather(cache_kc[l]), gather(cache_vc[l]), lp, lam_init)
        rows_p.append(r_p)
        rows_s.append(r_s)
    kb_p, vb_p, kc_p, vc_p, pool_p = [jnp.stack(z) for z in zip(*rows_p)]
    kb_s, vb_s, kc_s, vc_s, pool_s = [jnp.stack(z) for z in zip(*rows_s)]
    return (yp, ys, kb_p, vb_p, kc_p, vc_p, pool_p, kb_s, vb_s, kc_s, vc_s, pool_s)
```

```python
import functools
import math

import numpy as np
import jax
import jax.numpy as jnp
from jax import lax
from jax.experimental import pallas as pl
from jax.experimental.pallas import tpu as pltpu

F32 = jnp.float32
BF16 = jnp.bfloat16

HEAD_DIM = 64
ROT_DIM = HEAD_DIM // 4
ROPE_THETA = 500000.0
POOL_WINDOWS = (2, 4, 8, 16)
POOL_STATE = max(POOL_WINDOWS) - 1
N_GROUPS = 4
EXPERTS_PER_GROUP = 4
N_EXPERTS = N_GROUPS * EXPERTS_PER_GROUP
EPS = 1e-6
LANES = 128
ROUTE_OFF = N_GROUPS
VMEM_LIMIT = 56 * 1024 * 1024
NEG = -1e30


def _dot(a, b):
    return jnp.dot(a, b, preferred_element_type=F32)


def _dot_nt(a, b):
    return lax.dot_general(a, b, (((1,), (1,)), ((), ())), preferred_element_type=F32)


def _params(sem):
    return pltpu.CompilerParams(dimension_semantics=sem, vmem_limit_bytes=VMEM_LIMIT)


def _const_spec(shape):
    return pl.BlockSpec(shape, lambda *_: (0,) * len(shape))


def _in_proj_kernel(x_ref, g_ref, w_ref, qg_ref, kg_ref, gm_ref, c_ref, sa_ref, sb_ref,
                    a_out, q_out, k_out, v_out, qc_out, kc_out, vc_out, gate_out, *, sizes, cache_layout):
    x = x_ref[...]
    tm = x.shape[0]
    ms = jnp.mean(x * x, axis=-1, keepdims=True)
    h = ((x * lax.rsqrt(ms + EPS)) * g_ref[...]).astype(BF16)
    offs = np.concatenate([[0], np.cumsum(sizes)]).tolist()

    def seg(i, j=None):
        j = i if j is None else j
        return _dot(h, w_ref[:, offs[i]:offs[j + 1]])

    qk_w = sizes[1]
    reps = qk_w // LANES
    cos_t = jnp.tile(c_ref[...], (1, reps))
    sin_a = jnp.tile(sa_ref[...], (1, reps))
    sin_b = jnp.tile(sb_ref[...], (1, reps))
    half = ROT_DIM // 2

    def qk_norm_rope(y, gain):
        gms = _dot((y * y).astype(BF16), gm_ref[...])
        yn = (y * lax.rsqrt(gms + EPS)) * gain
        return (yn * cos_t + pltpu.roll(yn, qk_w - half, 1) * sin_a
                + pltpu.roll(yn, half, 1) * sin_b)

    def put_rows(ref, val):
        if cache_layout:
            for hd in range(reps):
                ref[pl.ds(hd, tm, stride=reps), :] = val[:, hd * LANES:(hd + 1) * LANES]
        else:
            ref[...] = val

    def put_cols(ref, val):
        ref[...] = val.T if cache_layout else val

    scale = HEAD_DIM ** -0.5
    a_out[...] = seg(0)
    q_out[...] = (qk_norm_rope(seg(1), qg_ref[...]) * scale).astype(q_out.dtype)
    put_rows(k_out, qk_norm_rope(seg(2), kg_ref[...]))
    put_rows(v_out, seg(3))
    qc_out[...] = (seg(4) * scale).astype(qc_out.dtype)
    put_cols(kc_out, seg(5))
    put_cols(vc_out, seg(6))
    gate_out[...] = jax.nn.sigmoid(seg(7, 9)).astype(gate_out.dtype)


def _in_proj(x, g1, w_in, qg, kg, gm, rope_tabs, pos_block, sizes, tm, seq_len, cache_layout):
    n, d = x.shape
    cos_t, sin_a, sin_b = rope_tabs
    row = lambda i: (i, 0)
    tab = pl.BlockSpec((tm, LANES), lambda i: (pos_block(i), 0))
    diff_w, stick_w = sizes[1], sizes[4]
    heads = diff_w // LANES

    def plain(w, dt):
        return pl.BlockSpec((tm, w), row), jax.ShapeDtypeStruct((n, w), dt)

    if cache_layout:
        per_seq = seq_len // tm
        rows = (pl.BlockSpec((tm * heads, LANES), row), jax.ShapeDtypeStruct((n * heads, LANES), F32))
        cols = (pl.BlockSpec((stick_w, tm), lambda i: (i // per_seq, i % per_seq)),
                jax.ShapeDtypeStruct((n // seq_len * stick_w, seq_len), F32))
    else:
        rows, cols = plain(diff_w, F32), plain(stick_w, F32)
    outs = [plain(sizes[0], F32), plain(diff_w, BF16), rows, rows,
            plain(stick_w, BF16), cols, cols, plain(sum(sizes[7:]), BF16)]
    return pl.pallas_call(
        functools.partial(_in_proj_kernel, sizes=tuple(sizes), cache_layout=cache_layout),
        grid=(n // tm,),
        in_specs=[pl.BlockSpec((tm, d), row), _const_spec((1, d)), _const_spec(w_in.shape),
                  _const_spec(qg.shape), _const_spec(kg.shape), _const_spec(gm.shape), tab, tab, tab],
        out_specs=[o[0] for o in outs],
        out_shape=[o[1] for o in outs],
        compiler_params=_params(("parallel",)),
        name="in_proj",
    )(x, g1, w_in, qg, kg, gm, cos_t, sin_a, sin_b)


def _pool_kernel(a_ref, pre_ref, w_ref, s_ref, o_ref, u_scr, *, pos0, chunk):
    t, width = a_ref.shape
    pad = POOL_STATE + 1
    u_scr[0:1, :] = jnp.zeros((1, width), F32)
    u_scr[1:pad, :] = pre_ref[...]
    u_scr[pad:pad + t, :] = a_ref[...]
    group = width // len(POOL_WINDOWS)
    lane_g = lax.broadcasted_iota(jnp.int32, (1, width), 1) // group
    for c0 in range(0, t, chunk):
        a = u_scr[pad + c0:pad + c0 + chunk, :]
        pos = pos0 + c0 + lax.broadcasted_iota(jnp.int32, (chunk, 1), 0)
        acc = a
        pooled = jnp.zeros_like(a)
        for i in range(1, POOL_STATE + 1):
            acc = acc + u_scr[pad + c0 - i:pad + c0 - i + chunk, :]
            if i + 1 in POOL_WINDOWS:
                g = POOL_WINDOWS.index(i + 1)
                cnt = jnp.minimum(i + 1, pos + 1).astype(F32)
                pooled = jnp.where(lane_g == g, acc / cnt, pooled)
        pooled = pooled - a
        o_ref[c0:c0 + chunk, :] = (_dot(pooled.astype(BF16), w_ref[...]) * s_ref[...]).astype(o_ref.dtype)


def _pool(a_in, prefix, w_bd, scale, b, t, pos0):
    width = a_in.shape[1]
    chunk = min(t, 256)
    return pl.pallas_call(
        functools.partial(_pool_kernel, pos0=pos0, chunk=chunk),
        grid=(b,),
        in_specs=[pl.BlockSpec((t, width), lambda i: (i, 0)),
                  pl.BlockSpec((None, POOL_STATE, width), lambda i: (i, 0, 0)),
                  _const_spec(w_bd.shape), _const_spec(scale.shape)],
        out_specs=pl.BlockSpec((t, width), lambda i: (i, 0)),
        out_shape=jax.ShapeDtypeStruct(a_in.shape, BF16),
        scratch_shapes=[pltpu.VMEM((t + POOL_STATE + 1, width), F32)],
        compiler_params=_params(("parallel",)),
        name="pool",
    )(a_in, prefix, w_bd, scale)


def _lambda(lq1, lk1, lq2, lk2, lam_init):
    s1 = jnp.sum(lq1[...] * lk1[...], axis=-1, keepdims=True)
    s2 = jnp.sum(lq2[...] * lk2[...], axis=-1, keepdims=True)
    return jnp.exp(s1) - jnp.exp(s2) + lam_init


def _sub_norm(o, gain, lam_init):
    ms = jnp.mean(o * o, axis=-1, keepdims=True)
    return (o * lax.rsqrt(ms + EPS)) * gain * (1.0 - lam_init)


def _softplus(z):
    return jnp.maximum(z, 0.0) + jnp.log(1.0 + jnp.exp(-jnp.abs(z)))


def _split_bf16(x):
    hi = x.astype(BF16)
    lo = (x - hi.astype(F32)).astype(BF16)
    return hi, lo


def _pad_rows(x, rows):
    return jnp.concatenate([x, jnp.zeros((rows - x.shape[0], x.shape[1]), x.dtype)], axis=0)


def _diff_attn_kernel(q_ref, k_ref, v_ref, lq1, lk1, lq2, lk2, sg_ref, o_ref, *, tq, heads, lam_init):
    h = pl.program_id(1)
    qi = pl.program_id(2)
    q = q_ref[...]
    lane = lax.broadcasted_iota(jnp.int32, (1, 2 * HEAD_DIM), 1)
    zero = jnp.zeros_like(q)
    qs = (jnp.where(lane < HEAD_DIM, q, zero), jnp.where(lane >= HEAD_DIM, q, zero))
    row = lax.broadcasted_iota(jnp.int32, (tq, tq), 0)
    col = lax.broadcasted_iota(jnp.int32, (tq, tq), 1)

    def step(kb, carry, diag):
        start = pl.multiple_of(kb * (tq * heads), tq * heads) + h
        k = k_ref[pl.ds(start, tq, stride=heads), :].astype(BF16)
        v = v_ref[pl.ds(start, tq, stride=heads), :].astype(BF16)
        out = []
        for c in range(2):
            m, l, acc = carry[c]
            s = _dot_nt(qs[c], k)
            if diag:
                s = jnp.where(col <= row, s, NEG)
            m_new = jnp.maximum(m, jnp.max(s, axis=-1, keepdims=True))
            alpha = jnp.exp(m - m_new)
            p = jnp.exp(s - m_new)
            l = alpha * l + jnp.sum(p, axis=-1, keepdims=True)
            acc = alpha * acc + _dot(p.astype(BF16), v)
            out.append((m_new, l, acc))
        return tuple(out)

    init1 = (jnp.full((tq, 1), NEG, F32), jnp.zeros((tq, 1), F32), jnp.zeros((tq, 2 * HEAD_DIM), F32))
    carry = lax.fori_loop(0, qi, lambda kb, c: step(kb, c, False), (init1, init1))
    (_, l1, a1), (_, l2, a2) = step(qi, carry, True)
    lam = _lambda(lq1, lk1, lq2, lk2, lam_init)
    o = a1 / l1 - lam * (a2 / l2)
    o_ref[...] = _sub_norm(o, sg_ref[...], lam_init).astype(o_ref.dtype)


def _diff_attn_prompt(q, k, v, lam_vecs, sg, b, t, lam_init, tq):
    n, w = q.shape
    hw = 2 * HEAD_DIM
    heads = w // hw
    nq = t // tq
    vec = _const_spec((1, HEAD_DIM))
    kv = pl.BlockSpec((t * heads, hw), lambda bi, h, i: (bi, 0))
    return pl.pallas_call(
        functools.partial(_diff_attn_kernel, tq=tq, heads=heads, lam_init=lam_init),
        grid=(b, heads, nq),
        in_specs=[pl.BlockSpec((tq, hw), lambda bi, h, i: (bi * nq + i, h)), kv, kv,
                  vec, vec, vec, vec, _const_spec((1, hw))],
        out_specs=pl.BlockSpec((tq, hw), lambda bi, h, i: (bi * nq + i, h)),
        out_shape=jax.ShapeDtypeStruct((n, w), BF16),
        compiler_params=_params(("parallel", "arbitrary", "arbitrary")),
        name="diff_attn_prompt",
    )(q, k, v, *lam_vecs, sg)


def _stick_kernel(q_ref, k_ref, v_ref, o_ref, *, tq):
    qi = pl.program_id(1)
    q = q_ref[...]
    w = q.shape[1]
    lane_h = lax.broadcasted_iota(jnp.int32, (1, w), 1) // HEAD_DIM
    row = lax.broadcasted_iota(jnp.int32, (tq, tq), 0)
    col = lax.broadcasted_iota(jnp.int32, (tq, tq), 1)
    tri = jnp.where(row > col, 1.0, 0.0).astype(BF16)
    o = jnp.zeros((tq, w), F32)
    for hh in range(w // HEAD_DIM):
        qh = jnp.where(lane_h == hh, q, jnp.zeros_like(q))

        def step(kb, carry, diag, qh=qh):
            acc, run = carry
            start = pl.multiple_of(kb * tq, tq)
            k = k_ref[:, pl.ds(start, tq)].astype(BF16)
            v = v_ref[:, pl.ds(start, tq)].astype(BF16)
            z = _dot(qh, k)
            sp = _softplus(z)
            lk = -sp
            if diag:
                lk = jnp.where(col < row, lk, 0.0)
            hi, lo = _split_bf16(lk)
            later = _dot(hi, tri) + _dot(lo, tri) + run
            a = jnp.exp(z - sp + later)
            if diag:
                a = jnp.where(col < row, a, 0.0)
            acc = acc + _dot_nt(a.astype(BF16), v)
            run = run + jnp.sum(lk, axis=-1, keepdims=True)
            return acc, run

        carry = step(qi, (jnp.zeros((tq, w), F32), jnp.zeros((tq, 1), F32)), True)
        acc, _ = lax.fori_loop(0, qi, lambda j, c: step(qi - 1 - j, c, False), carry)
        o = jnp.where(lane_h == hh, acc, o)
    o_ref[...] = o.astype(o_ref.dtype)


def _stick_prompt(q, k_t, v_t, b, t, tq):
    n, w = q.shape
    nq = t // tq
    kv = pl.BlockSpec((w, t), lambda bi, i: (bi, 0))
    return pl.pallas_call(
        functools.partial(_stick_kernel, tq=tq),
        grid=(b, nq),
        in_specs=[pl.BlockSpec((tq, w), lambda bi, i: (bi * nq + i, 0)), kv, kv],
        out_specs=pl.BlockSpec((tq, w), lambda bi, i: (bi * nq + i, 0)),
        out_shape=jax.ShapeDtypeStruct((n, w), BF16),
        compiler_params=_params(("parallel", "arbitrary")),
        name="stick_prompt",
    )(q, k_t, v_t)


def _head_rows(x, heads):
    return jnp.concatenate([x[:, h * LANES:(h + 1) * LANES] for h in range(heads)], axis=0)


def _diff_decode_kernel(pt_ref, q_ref, kn_ref, vn_ref, *rest, n_pages, lam_init):
    del pt_ref
    kp = rest[:n_pages]
    vp = rest[n_pages:2 * n_pages]
    lq1, lk1, lq2, lk2, sg_ref, o_ref, s_scr = rest[2 * n_pages:]
    t, w = q_ref.shape
    hw = 2 * HEAD_DIM
    heads = w // hw
    prow = kp[0].shape[0]
    nq = 2 * heads * t
    q = q_ref[...].astype(F32)
    lane = lax.broadcasted_iota(jnp.int32, (1, hw), 1)
    pieces = []
    for h in range(heads):
        qh = q[:, h * hw:(h + 1) * hw]
        pieces += [jnp.where(lane < HEAD_DIM, qh, 0.0), jnp.where(lane >= HEAD_DIM, qh, 0.0)]
    qrows = jnp.concatenate(pieces, axis=0).astype(BF16)
    row_h = lax.broadcasted_iota(jnp.int32, (nq, prow), 0) // (2 * t)
    col_h = lax.broadcasted_iota(jnp.int32, (nq, prow), 1) % heads
    page_mask = row_h == col_h
    for j in range(n_pages):
        s = _dot_nt(qrows, kp[j][...].astype(BF16))
        s_scr[:, j * prow:(j + 1) * prow] = jnp.where(page_mask, s, NEG)
    k_new = _pad_rows(_head_rows(kn_ref[...], heads), LANES).astype(BF16)
    v_new = _pad_rows(_head_rows(vn_ref[...], heads), LANES).astype(BF16)
    r = lax.broadcasted_iota(jnp.int32, (nq, LANES), 0)
    c = lax.broadcasted_iota(jnp.int32, (nq, LANES), 1)
    new_mask = (c < heads * t) & (r // (2 * t) == c // t) & (c % t <= r % t)
    s_scr[:, n_pages * prow:] = jnp.where(new_mask, _dot_nt(qrows, k_new), NEG)
    m = jnp.max(s_scr[...], axis=-1, keepdims=True)
    acc = jnp.zeros((nq, hw), F32)
    den = jnp.zeros((nq, 1), F32)
    for j in range(n_pages + 1):
        width = prow if j < n_pages else LANES
        p = jnp.exp(s_scr[:, j * prow:j * prow + width] - m)
        den = den + jnp.sum(p, axis=-1, keepdims=True)
        v = vp[j][...].astype(BF16) if j < n_pages else v_new
        acc = acc + _dot(p.astype(BF16), v)
    o = acc / den
    lam = _lambda(lq1, lk1, lq2, lk2, lam_init)
    for h in range(heads):
        r0 = 2 * h * t
        oh = o[r0:r0 + t] - lam * o[r0 + t:r0 + 2 * t]
        o_ref[:, h * hw:(h + 1) * hw] = _sub_norm(oh, sg_ref[...], lam_init).astype(o_ref.dtype)


def _page_specs(cache, layer, n_pages):
    blk = (None, None) + cache.shape[2:]
    return [pl.BlockSpec(blk, lambda b, pt, j=j: (layer, pt[b, j], 0, 0)) for j in range(n_pages)]


def _diff_decode(q, k_new, v_new, cache_k, cache_v, page_table, layer, lam_vecs, sg, t, lam_init):
    n, w = q.shape
    b, n_pages = page_table.shape
    prow = cache_k.shape[2]
    heads = w // (2 * HEAD_DIM)
    tok = pl.BlockSpec((t, w), lambda bi, pt: (bi, 0))
    vec = pl.BlockSpec((1, HEAD_DIM), lambda bi, pt: (0, 0))
    grid_spec = pltpu.PrefetchScalarGridSpec(
        num_scalar_prefetch=1,
        grid=(b,),
        in_specs=[tok, tok, tok] + _page_specs(cache_k, layer, n_pages) + _page_specs(cache_v, layer, n_pages)
        + [vec, vec, vec, vec, pl.BlockSpec((1, 2 * HEAD_DIM), lambda bi, pt: (0, 0))],
        out_specs=tok,
        scratch_shapes=[pltpu.VMEM((2 * heads * t, n_pages * prow + LANES), F32)],
    )
    return pl.pallas_call(
        functools.partial(_diff_decode_kernel, n_pages=n_pages, lam_init=lam_init),
        grid_spec=grid_spec,
        out_shape=jax.ShapeDtypeStruct((n, w), BF16),
        compiler_params=_params(("parallel",)),
        name="diff_decode",
    )(page_table, q, k_new, v_new, *([cache_k] * n_pages), *([cache_v] * n_pages), *lam_vecs, sg)


def _stick_decode_kernel(pt_ref, q_ref, kn_ref, vn_ref, *rest, n_pages):
    del pt_ref
    kp = rest[:n_pages]
    vp = rest[n_pages:2 * n_pages]
    o_ref = rest[2 * n_pages]
    t, w = q_ref.shape
    heads = w // HEAD_DIM
    page = kp[0].shape[1]
    nq = heads * t
    q = q_ref[...].astype(F32)
    lane_h = lax.broadcasted_iota(jnp.int32, (1, w), 1) // HEAD_DIM
    qrows = jnp.concatenate([jnp.where(lane_h == h, q, 0.0) for h in range(heads)], axis=0).astype(BF16)
    r = lax.broadcasted_iota(jnp.int32, (page, page), 0)
    c = lax.broadcasted_iota(jnp.int32, (page, page), 1)
    tri = jnp.where(r > c, 1.0, 0.0).astype(BF16)
    qr = lax.broadcasted_iota(jnp.int32, (nq, page), 0) % t
    kc = lax.broadcasted_iota(jnp.int32, (nq, page), 1)
    new_mask = kc < qr
    acc = jnp.zeros((nq, w), F32)
    run = jnp.zeros((nq, 1), F32)
    for j in range(n_pages, -1, -1):
        new = j == n_pages
        if new:
            z = _dot_nt(qrows, _pad_rows(kn_ref[...], page).astype(BF16))
        else:
            z = _dot(qrows, kp[j][...].astype(BF16))
        sp = _softplus(z)
        lk = -sp
        if new:
            lk = jnp.where(new_mask, lk, 0.0)
        hi, lo = _split_bf16(lk)
        later = _dot(hi, tri) + _dot(lo, tri) + run
        a = jnp.exp(z - sp + later)
        if new:
            a = jnp.where(new_mask, a, 0.0)
            acc = acc + _dot(a.astype(BF16), _pad_rows(vn_ref[...], page).astype(BF16))
        else:
            acc = acc + _dot_nt(a.astype(BF16), vp[j][...].astype(BF16))
        run = run + jnp.sum(lk, axis=-1, keepdims=True)
    o = jnp.zeros((t, w), F32)
    for h in range(heads):
        o = jnp.where(lane_h == h, acc[h * t:(h + 1) * t], o)
    o_ref[...] = o.astype(o_ref.dtype)


def _stick_decode(q, k_new, v_new, cache_k, cache_v, page_table, layer, t):
    n, w = q.shape
    b, n_pages = page_table.shape
    tok = pl.BlockSpec((t, w), lambda bi, pt: (bi, 0))
    grid_spec = pltpu.PrefetchScalarGridSpec(
        num_scalar_prefetch=1,
        grid=(b,),
        in_specs=[tok, tok, tok] + _page_specs(cache_k, layer, n_pages) + _page_specs(cache_v, layer, n_pages),
        out_specs=tok,
    )
    return pl.pallas_call(
        functools.partial(_stick_decode_kernel, n_pages=n_pages),
        grid_spec=grid_spec,
        out_shape=jax.ShapeDtypeStruct((n, w), BF16),
        compiler_params=_params(("parallel",)),
        name="stick_decode",
    )(page_table, q, k_new, v_new, *([cache_k] * n_pages), *([cache_v] * n_pages))


def _merge_kernel(x_ref, oa_ref, ob_ref, oc_ref, gate_ref, wa_ref, wb_ref, wc_ref, wo_ref,
                  g2_ref, wr_hi_ref, wr_lo_ref, br_ref, x_out, h_out, comb_out):
    d = x_ref.shape[1]
    gate = gate_ref[...].astype(F32)
    merged = (gate[:, 0:d] * _dot(oa_ref[...], wa_ref[...])
              + gate[:, d:2 * d] * _dot(ob_ref[...], wb_ref[...])
              + gate[:, 2 * d:3 * d] * _dot(oc_ref[...], wc_ref[...]))
    x = x_ref[...] + _dot(merged.astype(BF16), wo_ref[...])
    x_out[...] = x
    ms = jnp.mean(x * x, axis=-1, keepdims=True)
    h = (x * lax.rsqrt(ms + EPS)) * g2_ref[...]
    h_out[...] = h.astype(h_out.dtype)
    h_hi, h_lo = _split_bf16(h)
    logits = (_dot(h_hi, wr_hi_ref[...]) + _dot(h_lo, wr_hi_ref[...]) + _dot(h_hi, wr_lo_ref[...])
              + br_ref[...])
    lane = lax.broadcasted_iota(jnp.int32, logits.shape, 1).astype(F32)
    gl = jnp.where(lane < N_GROUPS, logits, NEG)
    gmax = jnp.max(gl, axis=-1, keepdims=True)
    gidx = jnp.min(jnp.where(gl == gmax, lane, 1e9), axis=-1, keepdims=True)
    gprob = 1.0 / jnp.sum(jnp.where(lane < N_GROUPS, jnp.exp(logits - gmax), 0.0), axis=-1, keepdims=True)
    rel = lane - ROUTE_OFF - EXPERTS_PER_GROUP * gidx
    el = jnp.where((rel >= 0) & (rel < EXPERTS_PER_GROUP), logits, NEG)
    t1 = jnp.max(el, axis=-1, keepdims=True)
    i1 = jnp.min(jnp.where(el == t1, lane, 1e9), axis=-1, keepdims=True)
    el2 = jnp.where(lane == i1, NEG, el)
    t2 = jnp.max(el2, axis=-1, keepdims=True)
    i2 = jnp.min(jnp.where(el2 == t2, lane, 1e9), axis=-1, keepdims=True)
    e21 = jnp.exp(t2 - t1)
    w1 = gprob / (1.0 + e21)
    w2 = w1 * e21
    comb_out[...] = jnp.where(lane == i1, w1, jnp.where(lane == i2, w2, 0.0))


def _merge(x, o_a, o_b, o_c, gates, wa, wb, wc, wo, g2, wr_hi, wr_lo, br, tm):
    n, d = x.shape
    row = lambda i: (i, 0)
    ins = [x, o_a, o_b, o_c, gates]
    consts = [wa, wb, wc, wo, g2, wr_hi, wr_lo, br]
    return pl.pallas_call(
        _merge_kernel,
        grid=(n // tm,),
        in_specs=[pl.BlockSpec((tm, a.shape[1]), row) for a in ins] + [_const_spec(c.shape) for c in consts],
        out_specs=[pl.BlockSpec((tm, d), row), pl.BlockSpec((tm, d), row), pl.BlockSpec((tm, LANES), row)],
        out_shape=[jax.ShapeDtypeStruct((n, d), F32), jax.ShapeDtypeStruct((n, d), BF16),
                   jax.ShapeDtypeStruct((n, LANES), F32)],
        compiler_params=_params(("parallel",)),
        name="merge",
    )(*ins, *consts)


def _moe_kernel(x_ref, h_ref, comb_ref, wgu_ref, wd_ref, o_ref, acc_ref):
    e = pl.program_id(1)
    f = wd_ref.shape[0]

    @pl.when(e == 0)
    def _():
        acc_ref[...] = jnp.zeros_like(acc_ref)

    gu = _dot(h_ref[...], wgu_ref[...])
    comb = comb_ref[...]
    lane = lax.broadcasted_iota(jnp.int32, comb.shape, 1)
    ce = jnp.sum(jnp.where(lane == e + ROUTE_OFF, comb, 0.0), axis=-1, keepdims=True)
    act = jax.nn.silu(gu[:, :f]) * gu[:, f:] * ce
    acc_ref[...] += _dot(act.astype(BF16), wd_ref[...])

    @pl.when(e == pl.num_programs(1) - 1)
    def _():
        o_ref[...] = x_ref[...] + acc_ref[...]


def _moe(x, h, comb, w_gu, w_d, tm):
    n, d = x.shape
    ne, _, f2 = w_gu.shape
    row = lambda i, e: (i, 0)
    return pl.pallas_call(
        _moe_kernel,
        grid=(n // tm, ne),
        in_specs=[pl.BlockSpec((tm, d), row), pl.BlockSpec((tm, d), row), pl.BlockSpec((tm, LANES), row),
                  pl.BlockSpec((None, d, f2), lambda i, e: (e, 0, 0)),
                  pl.BlockSpec((None, f2 // 2, d), lambda i, e: (e, 0, 0))],
        out_specs=pl.BlockSpec((tm, d), row),
        out_shape=jax.ShapeDtypeStruct((n, d), F32),
        scratch_shapes=[pltpu.VMEM((tm, d), F32)],
        compiler_params=_params(("parallel", "arbitrary")),
        name="moe",
    )(x, h, comb, w_gu, w_d)


def _rope_tables(positions):
    half = ROT_DIM // 2
    inv = ROPE_THETA ** (-jnp.arange(0, ROT_DIM, 2, dtype=F32) / ROT_DIM)
    ang = positions.astype(F32)[:, None] * inv[None, :]
    cos, sin = jnp.cos(ang), jnp.sin(ang)
    n = positions.shape[0]
    rest = HEAD_DIM - ROT_DIM
    comp_c = jnp.concatenate([cos, cos, jnp.ones((n, rest), F32)], axis=1)
    comp_a = jnp.concatenate([-sin, jnp.zeros((n, half + rest), F32)], axis=1)
    comp_b = jnp.concatenate([jnp.zeros((n, half), F32), sin, jnp.zeros((n, rest), F32)], axis=1)
    return tuple(jnp.tile(c, (1, LANES // HEAD_DIM)) for c in (comp_c, comp_a, comp_b))


def _block_diag(blocks):
    g, a, b = blocks.shape
    out = jnp.zeros((g * a, g * b), blocks.dtype)
    for i in range(g):
        out = out.at[i * a:(i + 1) * a, i * b:(i + 1) * b].set(blocks[i])
    return out


def kernel(x_prompt, x_sample, cache_kb, cache_vb, cache_kc, cache_vc, state_pool, page_table, norm1_g, w_in, pool_w, pool_scale, qn_g, kn_g, lam_q1, lam_k1, lam_q2, lam_k2, subln_g, w_proj_a, w_proj_b, w_proj_c, w_out, norm2_g, w_group, b_group, w_router, b_router, w_e_gate, w_e_up, w_e_down):
    bp, tp, d = x_prompt.shape
    bs, ts, _ = x_sample.shape
    depth = w_in.shape[0]
    n_pages, page = page_table.shape[1], cache_kb.shape[2]
    past_len = n_pages * page
    pool_width = pool_w.shape[1] * pool_w.shape[2]
    diff_heads, stick_heads = cache_kb.shape[3], cache_kc.shape[3]
    diff_w = diff_heads * 2 * HEAD_DIM
    stick_w = stick_heads * HEAD_DIM
    sizes = (pool_width, diff_w, diff_w, diff_w, stick_w, stick_w, stick_w, d, d, d)

    tm_p = min(256, tp)
    tm_s = min(256, bs * ts)
    tq = min(256, tp)
    tabs_p = _rope_tables(jnp.arange(tp))
    tabs_s = _rope_tables(past_len + (jnp.arange(tm_s) % ts))
    pos_block_p = lambda i: i % (tp // tm_p)
    pos_block_s = lambda i: 0

    gm = _block_diag(jnp.full((diff_w // HEAD_DIM, HEAD_DIM, HEAD_DIM), 1.0 / HEAD_DIM, F32)).astype(BF16)
    n_pool_pages = cache_kb.shape[1]
    ckb, cvb = (c.reshape(depth, n_pool_pages, page * diff_heads, 2 * HEAD_DIM) for c in (cache_kb, cache_vb))
    ckc, cvc = (c.transpose(0, 1, 3, 4, 2).reshape(depth, n_pool_pages, stick_w, page) for c in (cache_kc, cache_vc))
    zero_pool = jnp.zeros((bp, POOL_STATE, pool_width), F32)

    xp = x_prompt.reshape(bp * tp, d)
    xs = x_sample.reshape(bs * ts, d)
    rows_p, rows_s = [], []
    for l in range(depth):
        lam_init = 0.8 - 0.6 * math.exp(-0.3 * l)
        g1 = norm1_g[l].reshape(1, d)
        w_in_l = w_in[l].astype(BF16)
        qg = jnp.tile(qn_g[l].reshape(1, -1), (1, diff_heads))
        kg = jnp.tile(kn_g[l].reshape(1, -1), (1, diff_heads))
        w_bd = _block_diag(pool_w[l]).astype(BF16)
        p_scale = pool_scale[l].reshape(1, -1)
        lam_vecs = tuple(v[l].reshape(1, -1) for v in (lam_q1, lam_k1, lam_q2, lam_k2))
        sg = subln_g[l].reshape(1, -1)
        wa, wb, wc, wo = (w[l].astype(BF16) for w in (w_proj_a, w_proj_b, w_proj_c, w_out))
        g2 = norm2_g[l].reshape(1, d)
        wr = jnp.concatenate([w_group[l], w_router[l],
                              jnp.zeros((d, LANES - N_GROUPS - N_EXPERTS), F32)], axis=1)
        wr_hi = wr.astype(BF16)
        wr_lo = (wr - wr_hi.astype(F32)).astype(BF16)
        br = jnp.concatenate([b_group[l], b_router[l],
                              jnp.zeros((LANES - N_GROUPS - N_EXPERTS,), F32)]).reshape(1, LANES)
        w_gu = jnp.concatenate([w_e_gate[l], w_e_up[l]], axis=2).astype(BF16)
        w_d = w_e_down[l].astype(BF16)

        def tail(x, o_a, o_b, o_c, gates, tm):
            x_mid, h2, comb = _merge(x, o_a, o_b, o_c, gates, wa, wb, wc, wo, g2, wr_hi, wr_lo, br, tm)
            return _moe(x_mid, h2, comb, w_gu, w_d, tm)

        a_in, qb, kb, vb, qc, kc, vc, gates = _in_proj(
            xp, g1, w_in_l, qg, kg, gm, tabs_p, pos_block_p, sizes, tm_p, tp, True)
        o_a = _pool(a_in, zero_pool, w_bd, p_scale, bp, tp, 0)
        o_b = _diff_attn_prompt(qb, kb, vb, lam_vecs, sg, bp, tp, lam_init, tq)
        o_c = _stick_prompt(qc, kc, vc, bp, tp, tq)
        xp = tail(xp, o_a, o_b, o_c, gates, tm_p)
        rows_p.append((kb, vb, kc, vc, a_in.reshape(bp, tp, -1)[:, tp - POOL_STATE:]))

        a_in, qb, kb, vb, qc, kc, vc, gates = _in_proj(
            xs, g1, w_in_l, qg, kg, gm, tabs_s, pos_block_s, sizes, tm_s, ts, False)
        o_a = _pool(a_in, state_pool[l], w_bd, p_scale, bs, ts, past_len)
        o_b = _diff_decode(qb, kb, vb, ckb, cvb, page_table, l, lam_vecs, sg, ts, lam_init)
        o_c = _stick_decode(qc, kc, vc, ckc, cvc, page_table, l, ts)
        xs = tail(xs, o_a, o_b, o_c, gates, tm_s)
        new_pool = jnp.concatenate([state_pool[l], a_in.reshape(bs, ts, -1)], axis=1)[:, -POOL_STATE:]
        rows_s.append((kb, vb, kc, vc, new_pool))

    kb_p, vb_p, kc_p, vc_p, pool_p = [jnp.stack(z) for z in zip(*rows_p)]
    kb_s, vb_s, kc_s, vc_s, pool_s = [jnp.stack(z) for z in zip(*rows_s)]
    diff_shape = lambda b, t: (depth, b, t, diff_heads, 2 * HEAD_DIM)
    stick_t = lambda z: z.reshape(depth, bp, stick_heads, HEAD_DIM, tp).transpose(0, 1, 4, 2, 3)
    return (xp.reshape(bp, tp, d), xs.reshape(bs, ts, d),
            kb_p.reshape(diff_shape(bp, tp)), vb_p.reshape(diff_shape(bp, tp)), stick_t(kc_p), stick_t(vc_p), pool_p,
            kb_s.reshape(diff_shape(bs, ts)), vb_s.reshape(diff_shape(bs, ts)),
            kc_s.reshape(depth, bs, ts, stick_heads, HEAD_DIM), vc_s.reshape(depth, bs, ts, stick_heads, HEAD_DIM),
            pool_s)
```

```python
import functools
import math

import numpy as np
import jax
import jax.numpy as jnp
from jax import lax
from jax.experimental import pallas as pl
from jax.experimental.pallas import tpu as pltpu

F32 = jnp.float32
BF16 = jnp.bfloat16

HEAD_DIM = 64
ROT_DIM = HEAD_DIM // 4
ROPE_THETA = 500000.0
POOL_WINDOWS = (2, 4, 8, 16)
POOL_STATE = max(POOL_WINDOWS) - 1
N_GROUPS = 4
EXPERTS_PER_GROUP = 4
N_EXPERTS = N_GROUPS * EXPERTS_PER_GROUP
EPS = 1e-6
LANES = 128
ROUTE_OFF = N_GROUPS
VMEM_LIMIT = 56 * 1024 * 1024
NEG = -1e30
LOG2E = math.log2(math.e)


def _dot(a, b):
    return jnp.dot(a, b, preferred_element_type=F32)


def _dot_nt(a, b):
    return lax.dot_general(a, b, (((1,), (1,)), ((), ())), preferred_element_type=F32)


def _params(sem):
    return pltpu.CompilerParams(dimension_semantics=sem, vmem_limit_bytes=VMEM_LIMIT)


def _const_spec(shape):
    return pl.BlockSpec(shape, lambda *_: (0,) * len(shape))


def _in_proj_kernel(x_ref, g_ref, w_ref, qg_ref, kg_ref, gm_ref, c_ref, sa_ref, sb_ref,
                    a_out, q_out, k_out, v_out, qc_out, kc_out, vc_out, gate_out, *, sizes, cache_layout):
    x = x_ref[...]
    tm = x.shape[0]
    ms = jnp.mean(x * x, axis=-1, keepdims=True)
    h = ((x * lax.rsqrt(ms + EPS)) * g_ref[...]).astype(BF16)
    offs = np.concatenate([[0], np.cumsum(sizes)]).tolist()

    def seg(i, j=None):
        j = i if j is None else j
        return _dot(h, w_ref[:, offs[i]:offs[j + 1]])

    qk_w = sizes[1]
    reps = qk_w // LANES
    cos_t = jnp.tile(c_ref[...], (1, reps))
    sin_a = jnp.tile(sa_ref[...], (1, reps))
    sin_b = jnp.tile(sb_ref[...], (1, reps))
    half = ROT_DIM // 2

    def qk_norm_rope(y, gain):
        gms = _dot((y * y).astype(BF16), gm_ref[...])
        yn = (y * lax.rsqrt(gms + EPS)) * gain
        return (yn * cos_t + pltpu.roll(yn, qk_w - half, 1) * sin_a
                + pltpu.roll(yn, half, 1) * sin_b)

    def put_rows(ref, val):
        if cache_layout:
            for hd in range(reps):
                ref[pl.ds(hd, tm, stride=reps), :] = val[:, hd * LANES:(hd + 1) * LANES]
        else:
            ref[...] = val

    def put_cols(ref, val):
        ref[...] = val.T if cache_layout else val

    scale = HEAD_DIM ** -0.5
    a_out[...] = seg(0)
    q_out[...] = (qk_norm_rope(seg(1), qg_ref[...]) * (scale * LOG2E)).astype(q_out.dtype)
    put_rows(k_out, qk_norm_rope(seg(2), kg_ref[...]))
    put_rows(v_out, seg(3))
    qc_out[...] = (seg(4) * scale).astype(qc_out.dtype)
    put_cols(kc_out, seg(5))
    put_cols(vc_out, seg(6))
    gate_out[...] = jax.nn.sigmoid(seg(7, 9)).astype(gate_out.dtype)


def _in_proj(x, g1, w_in, qg, kg, gm, rope_tabs, pos_block, sizes, tm, seq_len, cache_layout):
    n, d = x.shape
    cos_t, sin_a, sin_b = rope_tabs
    row = lambda i: (i, 0)
    tab = pl.BlockSpec((tm, LANES), lambda i: (pos_block(i), 0))
    diff_w, stick_w = sizes[1], sizes[4]
    heads = diff_w // LANES

    def plain(w, dt):
        return pl.BlockSpec((tm, w), row), jax.ShapeDtypeStruct((n, w), dt)

    if cache_layout:
        per_seq = seq_len // tm
        rows = (pl.BlockSpec((tm * heads, LANES), row), jax.ShapeDtypeStruct((n * heads, LANES), F32))
        cols = (pl.BlockSpec((stick_w, tm), lambda i: (i // per_seq, i % per_seq)),
                jax.ShapeDtypeStruct((n // seq_len * stick_w, seq_len), F32))
    else:
        rows, cols = plain(diff_w, F32), plain(stick_w, F32)
    outs = [plain(sizes[0], F32), plain(diff_w, BF16), rows, rows,
            plain(stick_w, BF16), cols, cols, plain(sum(sizes[7:]), BF16)]
    return pl.pallas_call(
        functools.partial(_in_proj_kernel, sizes=tuple(sizes), cache_layout=cache_layout),
        grid=(n // tm,),
        in_specs=[pl.BlockSpec((tm, d), row), _const_spec((1, d)), _const_spec(w_in.shape),
                  _const_spec(qg.shape), _const_spec(kg.shape), _const_spec(gm.shape), tab, tab, tab],
        out_specs=[o[0] for o in outs],
        out_shape=[o[1] for o in outs],
        compiler_params=_params(("parallel",)),
        name="in_proj",
    )(x, g1, w_in, qg, kg, gm, cos_t, sin_a, sin_b)


def _pool_kernel(a_ref, pre_ref, w_ref, s_ref, o_ref, u_scr, *, pos0, chunk):
    t, width = a_ref.shape
    pad = POOL_STATE + 1
    u_scr[0:1, :] = jnp.zeros((1, width), F32)
    u_scr[1:pad, :] = pre_ref[...]
    u_scr[pad:pad + t, :] = a_ref[...]
    group = width // len(POOL_WINDOWS)
    lane_g = lax.broadcasted_iota(jnp.int32, (1, width), 1) // group
    for c0 in range(0, t, chunk):
        a = u_scr[pad + c0:pad + c0 + chunk, :]
        pos = pos0 + c0 + lax.broadcasted_iota(jnp.int32, (chunk, 1), 0)
        acc = a
        pooled = jnp.zeros_like(a)
        for i in range(1, POOL_STATE + 1):
            acc = acc + u_scr[pad + c0 - i:pad + c0 - i + chunk, :]
            if i + 1 in POOL_WINDOWS:
                g = POOL_WINDOWS.index(i + 1)
                cnt = jnp.minimum(i + 1, pos + 1).astype(F32)
                pooled = jnp.where(lane_g == g, acc / cnt, pooled)
        pooled = pooled - a
        o_ref[c0:c0 + chunk, :] = (_dot(pooled.astype(BF16), w_ref[...]) * s_ref[...]).astype(o_ref.dtype)


def _pool(a_in, prefix, w_bd, scale, b, t, pos0):
    width = a_in.shape[1]
    chunk = min(t, 256)
    return pl.pallas_call(
        functools.partial(_pool_kernel, pos0=pos0, chunk=chunk),
        grid=(b,),
        in_specs=[pl.BlockSpec((t, width), lambda i: (i, 0)),
                  pl.BlockSpec((None, POOL_STATE, width), lambda i: (i, 0, 0)),
                  _const_spec(w_bd.shape), _const_spec(scale.shape)],
        out_specs=pl.BlockSpec((t, width), lambda i: (i, 0)),
        out_shape=jax.ShapeDtypeStruct(a_in.shape, BF16),
        scratch_shapes=[pltpu.VMEM((t + POOL_STATE + 1, width), F32)],
        compiler_params=_params(("parallel",)),
        name="pool",
    )(a_in, prefix, w_bd, scale)


def _lambda(lq1, lk1, lq2, lk2, lam_init):
    s1 = jnp.sum(lq1[...] * lk1[...], axis=-1, keepdims=True)
    s2 = jnp.sum(lq2[...] * lk2[...], axis=-1, keepdims=True)
    return jnp.exp(s1) - jnp.exp(s2) + lam_init


def _sub_norm(o, gain, lam_init):
    ms = jnp.mean(o * o, axis=-1, keepdims=True)
    return (o * lax.rsqrt(ms + EPS)) * gain * (1.0 - lam_init)


def _softplus(z):
    return jnp.maximum(z, 0.0) + jnp.log(1.0 + jnp.exp(-jnp.abs(z)))


def _split_bf16(x):
    hi = x.astype(BF16)
    lo = (x - hi.astype(F32)).astype(BF16)
    return hi, lo


def _pad_rows(x, rows):
    return jnp.concatenate([x, jnp.zeros((rows - x.shape[0], x.shape[1]), x.dtype)], axis=0)


def _lane_chunks(x):
    return [x[:, i * LANES:(i + 1) * LANES] for i in range(x.shape[1] // LANES)]


def _diff_attn_kernel(q_ref, k_ref, v_ref, lq1, lk1, lq2, lk2, sg_ref, o_ref, s_scr, *, tq, tk, heads, lam_init):
    hp = pl.program_id(1)
    qi = pl.program_id(2)
    hw = 2 * HEAD_DIM
    hpb = q_ref.shape[1] // hw
    lane = lax.broadcasted_iota(jnp.int32, (1, hw), 1)
    qs = []
    for j in range(hpb):
        q = q_ref[:, j * hw:(j + 1) * hw]
        zero = jnp.zeros_like(q)
        qs += [jnp.where(lane < HEAD_DIM, q, zero), jnp.where(lane >= HEAD_DIM, q, zero)]
    streams = range(2 * hpb)
    n_blk = (qi * tq + tq + tk - 1) // tk
    last = n_blk - 1

    def rows(ref, kb, j):
        start = pl.multiple_of(kb * (tk * heads), tk * heads) + hp * hpb + j
        return ref[pl.ds(start, tk, stride=heads), :].astype(BF16)

    def scores(kb, mx, masked):
        ks = [rows(k_ref, kb, j) for j in range(hpb)]
        col0 = pl.multiple_of(kb * tk, tk)
        s = [_dot_nt(qs[i], ks[i // 2]) for i in streams]
        if masked:
            row = qi * tq + lax.broadcasted_iota(jnp.int32, (tq, tk), 0)
            col = col0 + lax.broadcasted_iota(jnp.int32, (tq, tk), 1)
            s = [jnp.where(col <= row, s[i], NEG) for i in streams]
        for i in streams:
            s_scr[i, :, pl.ds(col0, tk)] = s[i]
        return tuple(functools.reduce(jnp.maximum, _lane_chunks(s[i]), mx[i]) for i in streams)

    neg = jnp.full((tq, LANES), NEG, F32)
    mx = lax.fori_loop(0, last, lambda kb, m: scores(kb, m, False), (neg,) * len(streams))
    mx = scores(last, mx, True)
    m_b = [jnp.broadcast_to(jnp.max(m, axis=-1, keepdims=True), (tq, LANES)) for m in mx]

    def accumulate(kb, carry):
        vs = [rows(v_ref, kb, j) for j in range(hpb)]
        col0 = pl.multiple_of(kb * tk, tk)
        p = [[jnp.exp2(sc - m_b[i]) for sc in _lane_chunks(s_scr[i, :, pl.ds(col0, tk)])] for i in streams]
        den = [functools.reduce(jnp.add, p[i], carry[i][0]) for i in streams]
        acc = [carry[i][1] + _dot(jnp.concatenate(p[i], axis=1).astype(BF16), vs[i // 2]) for i in streams]
        return tuple((den[i], acc[i]) for i in streams)

    zero_c = (jnp.zeros((tq, LANES), F32), jnp.zeros((tq, hw), F32))
    res = lax.fori_loop(0, n_blk, accumulate, (zero_c,) * len(streams))
    lam = _lambda(lq1, lk1, lq2, lk2, lam_init)
    norm = [acc / jnp.sum(den, axis=-1, keepdims=True) for den, acc in res]
    for j in range(hpb):
        o = norm[2 * j] - lam * norm[2 * j + 1]
        o_ref[:, j * hw:(j + 1) * hw] = _sub_norm(o, sg_ref[...], lam_init).astype(o_ref.dtype)


def _diff_attn_prompt(q, k, v, lam_vecs, sg, b, t, lam_init, tq, tk, hpb):
    n, w = q.shape
    hw = 2 * HEAD_DIM
    heads = w // hw
    nq = t // tq
    vec = _const_spec((1, HEAD_DIM))
    kv = pl.BlockSpec((t * heads, hw), lambda bi, h, i: (bi, 0))
    return pl.pallas_call(
        functools.partial(_diff_attn_kernel, tq=tq, tk=tk, heads=heads, lam_init=lam_init),
        grid=(b, heads // hpb, nq),
        in_specs=[pl.BlockSpec((tq, hpb * hw), lambda bi, h, i: (bi * nq + i, h)), kv, kv,
                  vec, vec, vec, vec, _const_spec((1, hw))],
        out_specs=pl.BlockSpec((tq, hpb * hw), lambda bi, h, i: (bi * nq + i, h)),
        out_shape=jax.ShapeDtypeStruct((n, w), BF16),
        scratch_shapes=[pltpu.VMEM((2 * hpb, tq, t), F32)],
        compiler_params=_params(("parallel", "arbitrary", "arbitrary")),
        name="diff_attn_prompt",
    )(q, k, v, *lam_vecs, sg)


def _stick_kernel(q_ref, k_ref, v_ref, o_ref, *, tq):
    qi = pl.program_id(1)
    q = q_ref[...]
    w = q.shape[1]
    lane_h = lax.broadcasted_iota(jnp.int32, (1, w), 1) // HEAD_DIM
    row = lax.broadcasted_iota(jnp.int32, (tq, tq), 0)
    col = lax.broadcasted_iota(jnp.int32, (tq, tq), 1)
    tri = jnp.where(row > col, 1.0, 0.0).astype(BF16)
    heads = w // HEAD_DIM
    qh = [jnp.where(lane_h == hh, q, jnp.zeros_like(q)) for hh in range(heads)]

    def step(kb, carry, diag):
        start = pl.multiple_of(kb * tq, tq)
        k = k_ref[:, pl.ds(start, tq)].astype(BF16)
        v = v_ref[:, pl.ds(start, tq)].astype(BF16)
        hs = range(heads)
        z = [_dot(qh[hh], k) for hh in hs]
        sp = [_softplus(z[hh]) for hh in hs]
        lk = [-sp[hh] for hh in hs]
        if diag:
            lk = [jnp.where(col < row, lk[hh], 0.0) for hh in hs]
        parts = [_split_bf16(lk[hh]) for hh in hs]
        later = [_dot(parts[hh][0], tri) + _dot(parts[hh][1], tri) + carry[hh][1] for hh in hs]
        a = [jnp.exp(z[hh] - sp[hh] + later[hh]) for hh in hs]
        if diag:
            a = [jnp.where(col < row, a[hh], 0.0) for hh in hs]
        acc = [carry[hh][0] + _dot_nt(a[hh].astype(BF16), v[hh * HEAD_DIM:(hh + 1) * HEAD_DIM, :]) for hh in hs]
        run = [carry[hh][1] + jnp.sum(lk[hh], axis=-1, keepdims=True) for hh in hs]
        return tuple((acc[hh], run[hh]) for hh in hs)

    init = tuple((jnp.zeros((tq, HEAD_DIM), F32), jnp.zeros((tq, 1), F32)) for _ in range(heads))
    carry = step(qi, init, True)
    carry = lax.fori_loop(0, qi, lambda j, c: step(qi - 1 - j, c, False), carry)
    o_ref[...] = jnp.concatenate([c[0] for c in carry], axis=1).astype(o_ref.dtype)


def _stick_prompt(q, k_t, v_t, b, t, tq):
    n, w = q.shape
    nq = t // tq
    kv = pl.BlockSpec((w, t), lambda bi, i: (bi, 0))
    return pl.pallas_call(
        functools.partial(_stick_kernel, tq=tq),
        grid=(b, nq),
        in_specs=[pl.BlockSpec((tq, w), lambda bi, i: (bi * nq + i, 0)), kv, kv],
        out_specs=pl.BlockSpec((tq, w), lambda bi, i: (bi * nq + i, 0)),
        out_shape=jax.ShapeDtypeStruct((n, w), BF16),
        compiler_params=_params(("parallel", "arbitrary")),
        name="stick_prompt",
    )(q, k_t, v_t)


def _head_rows(x, heads):
    return jnp.concatenate([x[:, h * LANES:(h + 1) * LANES] for h in range(heads)], axis=0)


def _diff_decode_kernel(pt_ref, q_ref, kn_ref, vn_ref, *rest, n_pages, lam_init):
    del pt_ref
    kp = rest[:n_pages]
    vp = rest[n_pages:2 * n_pages]
    lq1, lk1, lq2, lk2, sg_ref, o_ref, s_scr = rest[2 * n_pages:]
    t, w = q_ref.shape
    hw = 2 * HEAD_DIM
    heads = w // hw
    prow = kp[0].shape[0]
    nq = 2 * heads * t
    q = q_ref[...].astype(F32)
    lane = lax.broadcasted_iota(jnp.int32, (1, hw), 1)
    pieces = []
    for h in range(heads):
        qh = q[:, h * hw:(h + 1) * hw]
        pieces += [jnp.where(lane < HEAD_DIM, qh, 0.0), jnp.where(lane >= HEAD_DIM, qh, 0.0)]
    qrows = jnp.concatenate(pieces, axis=0).astype(BF16)
    row_h = lax.broadcasted_iota(jnp.int32, (nq, prow), 0) // (2 * t)
    col_h = lax.broadcasted_iota(jnp.int32, (nq, prow), 1) % heads
    page_mask = row_h == col_h
    for j in range(n_pages):
        s = _dot_nt(qrows, kp[j][...].astype(BF16))
        s_scr[:, j * prow:(j + 1) * prow] = jnp.where(page_mask, s, NEG)
    k_new = _pad_rows(_head_rows(kn_ref[...], heads), LANES).astype(BF16)
    v_new = _pad_rows(_head_rows(vn_ref[...], heads), LANES).astype(BF16)
    r = lax.broadcasted_iota(jnp.int32, (nq, LANES), 0)
    c = lax.broadcasted_iota(jnp.int32, (nq, LANES), 1)
    new_mask = (c < heads * t) & (r // (2 * t) == c // t) & (c % t <= r % t)
    s_scr[:, n_pages * prow:] = jnp.where(new_mask, _dot_nt(qrows, k_new), NEG)
    m = jnp.max(s_scr[...], axis=-1, keepdims=True)
    acc = jnp.zeros((nq, hw), F32)
    den = jnp.zeros((nq, 1), F32)
    for j in range(n_pages + 1):
        width = prow if j < n_pages else LANES
        p = jnp.exp2(s_scr[:, j * prow:j * prow + width] - m)
        den = den + jnp.sum(p, axis=-1, keepdims=True)
        v = vp[j][...].astype(BF16) if j < n_pages else v_new
        acc = acc + _dot(p.astype(BF16), v)
    o = acc / den
    lam = _lambda(lq1, lk1, lq2, lk2, lam_init)
    for h in range(heads):
        r0 = 2 * h * t
        oh = o[r0:r0 + t] - lam * o[r0 + t:r0 + 2 * t]
        o_ref[:, h * hw:(h + 1) * hw] = _sub_norm(oh, sg_ref[...], lam_init).astype(o_ref.dtype)


def _page_specs(cache, layer, n_pages):
    blk = (None, None) + cache.shape[2:]
    return [pl.BlockSpec(blk, lambda b, pt, j=j: (layer, pt[b, j], 0, 0)) for j in range(n_pages)]


def _diff_decode(q, k_new, v_new, cache_k, cache_v, page_table, layer, lam_vecs, sg, t, lam_init):
    n, w = q.shape
    b, n_pages = page_table.shape
    prow = cache_k.shape[2]
    heads = w // (2 * HEAD_DIM)
    tok = pl.BlockSpec((t, w), lambda bi, pt: (bi, 0))
    vec = pl.BlockSpec((1, HEAD_DIM), lambda bi, pt: (0, 0))
    grid_spec = pltpu.PrefetchScalarGridSpec(
        num_scalar_prefetch=1,
        grid=(b,),
        in_specs=[tok, tok, tok] + _page_specs(cache_k, layer, n_pages) + _page_specs(cache_v, layer, n_pages)
        + [vec, vec, vec, vec, pl.BlockSpec((1, 2 * HEAD_DIM), lambda bi, pt: (0, 0))],
        out_specs=tok,
        scratch_shapes=[pltpu.VMEM((2 * heads * t, n_pages * prow + LANES), F32)],
    )
    return pl.pallas_call(
        functools.partial(_diff_decode_kernel, n_pages=n_pages, lam_init=lam_init),
        grid_spec=grid_spec,
        out_shape=jax.ShapeDtypeStruct((n, w), BF16),
        compiler_params=_params(("parallel",)),
        name="diff_decode",
    )(page_table, q, k_new, v_new, *([cache_k] * n_pages), *([cache_v] * n_pages), *lam_vecs, sg)


def _stick_decode_kernel(pt_ref, q_ref, kn_ref, vn_ref, *rest, n_pages):
    del pt_ref
    kp = rest[:n_pages]
    vp = rest[n_pages:2 * n_pages]
    o_ref = rest[2 * n_pages]
    t, w = q_ref.shape
    heads = w // HEAD_DIM
    page = kp[0].shape[1]
    nq = heads * t
    q = q_ref[...].astype(F32)
    lane_h = lax.broadcasted_iota(jnp.int32, (1, w), 1) // HEAD_DIM
    qrows = jnp.concatenate([jnp.where(lane_h == h, q, 0.0) for h in range(heads)], axis=0).astype(BF16)
    r = lax.broadcasted_iota(jnp.int32, (page, page), 0)
    c = lax.broadcasted_iota(jnp.int32, (page, page), 1)
    tri = jnp.where(r > c, 1.0, 0.0).astype(BF16)
    qr = lax.broadcasted_iota(jnp.int32, (nq, page), 0) % t
    kc = lax.broadcasted_iota(jnp.int32, (nq, page), 1)
    new_mask = kc < qr
    pages = range(n_pages + 1)
    z = [_dot(qrows, kp[j][...].astype(BF16)) for j in range(n_pages)]
    z.append(_dot_nt(qrows, _pad_rows(kn_ref[...], page).astype(BF16)))
    sp = [_softplus(z[j]) for j in pages]
    lk = [-sp[j] for j in pages]
    lk[n_pages] = jnp.where(new_mask, lk[n_pages], 0.0)
    parts = [_split_bf16(lk[j]) for j in pages]
    within = [_dot(parts[j][0], tri) + _dot(parts[j][1], tri) for j in pages]
    totals = [jnp.sum(lk[j], axis=-1, keepdims=True) for j in pages]
    run = jnp.zeros((nq, 1), F32)
    a = [None] * (n_pages + 1)
    for j in range(n_pages, -1, -1):
        a[j] = jnp.exp(z[j] - sp[j] + within[j] + run)
        run = run + totals[j]
    a[n_pages] = jnp.where(new_mask, a[n_pages], 0.0)
    acc = _dot(a[n_pages].astype(BF16), _pad_rows(vn_ref[...], page).astype(BF16))
    for j in range(n_pages):
        acc = acc + _dot_nt(a[j].astype(BF16), vp[j][...].astype(BF16))
    o = jnp.zeros((t, w), F32)
    for h in range(heads):
        o = jnp.where(lane_h == h, acc[h * t:(h + 1) * t], o)
    o_ref[...] = o.astype(o_ref.dtype)


def _stick_decode(q, k_new, v_new, cache_k, cache_v, page_table, layer, t):
    n, w = q.shape
    b, n_pages = page_table.shape
    tok = pl.BlockSpec((t, w), lambda bi, pt: (bi, 0))
    grid_spec = pltpu.PrefetchScalarGridSpec(
        num_scalar_prefetch=1,
        grid=(b,),
        in_specs=[tok, tok, tok] + _page_specs(cache_k, layer, n_pages) + _page_specs(cache_v, layer, n_pages),
        out_specs=tok,
    )
    return pl.pallas_call(
        functools.partial(_stick_decode_kernel, n_pages=n_pages),
        grid_spec=grid_spec,
        out_shape=jax.ShapeDtypeStruct((n, w), BF16),
        compiler_params=_params(("parallel",)),
        name="stick_decode",
    )(page_table, q, k_new, v_new, *([cache_k] * n_pages), *([cache_v] * n_pages))


def _merge_kernel(x_ref, oa_ref, ob_ref, oc_ref, gate_ref, wa_ref, wb_ref, wc_ref, wo_ref,
                  g2_ref, wr_hi_ref, wr_lo_ref, br_ref, x_out, h_out, comb_out):
    d = x_ref.shape[1]
    gate = gate_ref[...].astype(F32)
    merged = (gate[:, 0:d] * _dot(oa_ref[...], wa_ref[...])
              + gate[:, d:2 * d] * _dot(ob_ref[...], wb_ref[...])
              + gate[:, 2 * d:3 * d] * _dot(oc_ref[...], wc_ref[...]))
    x = x_ref[...] + _dot(merged.astype(BF16), wo_ref[...])
    x_out[...] = x
    ms = jnp.mean(x * x, axis=-1, keepdims=True)
    h = (x * lax.rsqrt(ms + EPS)) * g2_ref[...]
    h_out[...] = h.astype(h_out.dtype)
    h_hi, h_lo = _split_bf16(h)
    logits = (_dot(h_hi, wr_hi_ref[...]) + _dot(h_lo, wr_hi_ref[...]) + _dot(h_hi, wr_lo_ref[...])
              + br_ref[...])
    lane = lax.broadcasted_iota(jnp.int32, logits.shape, 1).astype(F32)
    gl = jnp.where(lane < N_GROUPS, logits, NEG)
    gmax = jnp.max(gl, axis=-1, keepdims=True)
    gidx = jnp.min(jnp.where(gl == gmax, lane, 1e9), axis=-1, keepdims=True)
    gprob = 1.0 / jnp.sum(jnp.where(lane < N_GROUPS, jnp.exp(logits - gmax), 0.0), axis=-1, keepdims=True)
    rel = lane - ROUTE_OFF - EXPERTS_PER_GROUP * gidx
    el = jnp.where((rel >= 0) & (rel < EXPERTS_PER_GROUP), logits, NEG)
    t1 = jnp.max(el, axis=-1, keepdims=True)
    i1 = jnp.min(jnp.where(el == t1, lane, 1e9), axis=-1, keepdims=True)
    el2 = jnp.where(lane == i1, NEG, el)
    t2 = jnp.max(el2, axis=-1, keepdims=True)
    i2 = jnp.min(jnp.where(el2 == t2, lane, 1e9), axis=-1, keepdims=True)
    e21 = jnp.exp(t2 - t1)
    w1 = gprob / (1.0 + e21)
    w2 = w1 * e21
    comb_out[...] = jnp.where(lane == i1, w1, jnp.where(lane == i2, w2, 0.0))


def _merge(x, o_a, o_b, o_c, gates, wa, wb, wc, wo, g2, wr_hi, wr_lo, br, tm):
    n, d = x.shape
    row = lambda i: (i, 0)
    ins = [x, o_a, o_b, o_c, gates]
    consts = [wa, wb, wc, wo, g2, wr_hi, wr_lo, br]
    return pl.pallas_call(
        _merge_kernel,
        grid=(n // tm,),
        in_specs=[pl.BlockSpec((tm, a.shape[1]), row) for a in ins] + [_const_spec(c.shape) for c in consts],
        out_specs=[pl.BlockSpec((tm, d), row), pl.BlockSpec((tm, d), row), pl.BlockSpec((tm, LANES), row)],
        out_shape=[jax.ShapeDtypeStruct((n, d), F32), jax.ShapeDtypeStruct((n, d), BF16),
                   jax.ShapeDtypeStruct((n, LANES), F32)],
        compiler_params=_params(("parallel",)),
        name="merge",
    )(*ins, *consts)


def _moe_kernel(x_ref, h_ref, comb_ref, wgu_ref, wd_ref, o_ref, acc_ref):
    e = pl.program_id(1)
    f = wd_ref.shape[0]

    @pl.when(e == 0)
    def _():
        acc_ref[...] = jnp.zeros_like(acc_ref)

    gu = _dot(h_ref[...], wgu_ref[...])
    comb = comb_ref[...]
    lane = lax.broadcasted_iota(jnp.int32, comb.shape, 1)
    ce = jnp.sum(jnp.where(lane == e + ROUTE_OFF, comb, 0.0), axis=-1, keepdims=True)
    act = jax.nn.silu(gu[:, :f]) * gu[:, f:] * ce
    acc_ref[...] += _dot(act.astype(BF16), wd_ref[...])

    @pl.when(e == pl.num_programs(1) - 1)
    def _():
        o_ref[...] = x_ref[...] + acc_ref[...]


def _moe(x, h, comb, w_gu, w_d, tm):
    n, d = x.shape
    ne, _, f2 = w_gu.shape
    row = lambda i, e: (i, 0)
    return pl.pallas_call(
        _moe_kernel,
        grid=(n // tm, ne),
        in_specs=[pl.BlockSpec((tm, d), row), pl.BlockSpec((tm, d), row), pl.BlockSpec((tm, LANES), row),
                  pl.BlockSpec((None, d, f2), lambda i, e: (e, 0, 0)),
                  pl.BlockSpec((None, f2 // 2, d), lambda i, e: (e, 0, 0))],
        out_specs=pl.BlockSpec((tm, d), row),
        out_shape=jax.ShapeDtypeStruct((n, d), F32),
        scratch_shapes=[pltpu.VMEM((tm, d), F32)],
        compiler_params=_params(("parallel", "arbitrary")),
        name="moe",
    )(x, h, comb, w_gu, w_d)


def _rope_tables(positions):
    half = ROT_DIM // 2
    inv = ROPE_THETA ** (-jnp.arange(0, ROT_DIM, 2, dtype=F32) / ROT_DIM)
    ang = positions.astype(F32)[:, None] * inv[None, :]
    cos, sin = jnp.cos(ang), jnp.sin(ang)
    n = positions.shape[0]
    rest = HEAD_DIM - ROT_DIM
    comp_c = jnp.concatenate([cos, cos, jnp.ones((n, rest), F32)], axis=1)
    comp_a = jnp.concatenate([-sin, jnp.zeros((n, half + rest), F32)], axis=1)
    comp_b = jnp.concatenate([jnp.zeros((n, half), F32), sin, jnp.zeros((n, rest), F32)], axis=1)
    return tuple(jnp.tile(c, (1, LANES // HEAD_DIM)) for c in (comp_c, comp_a, comp_b))


def _block_diag(blocks):
    g, a, b = blocks.shape
    out = jnp.zeros((g * a, g * b), blocks.dtype)
    for i in range(g):
        out = out.at[i * a:(i + 1) * a, i * b:(i + 1) * b].set(blocks[i])
    return out


def kernel(x_prompt, x_sample, cache_kb, cache_vb, cache_kc, cache_vc, state_pool, page_table, norm1_g, w_in, pool_w, pool_scale, qn_g, kn_g, lam_q1, lam_k1, lam_q2, lam_k2, subln_g, w_proj_a, w_proj_b, w_proj_c, w_out, norm2_g, w_group, b_group, w_router, b_router, w_e_gate, w_e_up, w_e_down):
    bp, tp, d = x_prompt.shape
    bs, ts, _ = x_sample.shape
    depth = w_in.shape[0]
    n_pages, page = page_table.shape[1], cache_kb.shape[2]
    past_len = n_pages * page
    pool_width = pool_w.shape[1] * pool_w.shape[2]
    diff_heads, stick_heads = cache_kb.shape[3], cache_kc.shape[3]
    diff_w = diff_heads * 2 * HEAD_DIM
    stick_w = stick_heads * HEAD_DIM
    sizes = (pool_width, diff_w, diff_w, diff_w, stick_w, stick_w, stick_w, d, d, d)

    tm_p = min(256, tp)
    tm_s = min(256, bs * ts)
    tq = min(256, tp)
    tk = min(512, tp)
    tm_moe_p = min(1024, bp * tp)
    tm_moe_s = min(1024, bs * ts)
    tabs_p = _rope_tables(jnp.arange(tp))
    tabs_s = _rope_tables(past_len + (jnp.arange(tm_s) % ts))
    pos_block_p = lambda i: i % (tp // tm_p)
    pos_block_s = lambda i: 0

    gm = _block_diag(jnp.full((diff_w // HEAD_DIM, HEAD_DIM, HEAD_DIM), 1.0 / HEAD_DIM, F32)).astype(BF16)
    n_pool_pages = cache_kb.shape[1]
    ckb, cvb = (c.reshape(depth, n_pool_pages, page * diff_heads, 2 * HEAD_DIM) for c in (cache_kb, cache_vb))
    ckc, cvc = (c.transpose(0, 1, 3, 4, 2).reshape(depth, n_pool_pages, stick_w, page) for c in (cache_kc, cache_vc))
    zero_pool = jnp.zeros((bp, POOL_STATE, pool_width), F32)

    xp = x_prompt.reshape(bp * tp, d)
    xs = x_sample.reshape(bs * ts, d)
    rows_p, rows_s = [], []
    for l in range(depth):
        lam_init = 0.8 - 0.6 * math.exp(-0.3 * l)
        g1 = norm1_g[l].reshape(1, d)
        w_in_l = w_in[l].astype(BF16)
        qg = jnp.tile(qn_g[l].reshape(1, -1), (1, diff_heads))
        kg = jnp.tile(kn_g[l].reshape(1, -1), (1, diff_heads))
        w_bd = _block_diag(pool_w[l]).astype(BF16)
        p_scale = pool_scale[l].reshape(1, -1)
        lam_vecs = tuple(v[l].reshape(1, -1) for v in (lam_q1, lam_k1, lam_q2, lam_k2))
        sg = subln_g[l].reshape(1, -1)
        wa, wb, wc, wo = (w[l].astype(BF16) for w in (w_proj_a, w_proj_b, w_proj_c, w_out))
        g2 = norm2_g[l].reshape(1, d)
        wr = jnp.concatenate([w_group[l], w_router[l],
                              jnp.zeros((d, LANES - N_GROUPS - N_EXPERTS), F32)], axis=1)
        wr_hi = wr.astype(BF16)
        wr_lo = (wr - wr_hi.astype(F32)).astype(BF16)
        br = jnp.concatenate([b_group[l], b_router[l],
                              jnp.zeros((LANES - N_GROUPS - N_EXPERTS,), F32)]).reshape(1, LANES)
        w_gu = jnp.concatenate([w_e_gate[l], w_e_up[l]], axis=2).astype(BF16)
        w_d = w_e_down[l].astype(BF16)

        def tail(x, o_a, o_b, o_c, gates, tm, tm_moe):
            x_mid, h2, comb = _merge(x, o_a, o_b, o_c, gates, wa, wb, wc, wo, g2, wr_hi, wr_lo, br, tm)
            return _moe(x_mid, h2, comb, w_gu, w_d, tm_moe)

        a_in, qb, kb, vb, qc, kc, vc, gates = _in_proj(
            xp, g1, w_in_l, qg, kg, gm, tabs_p, pos_block_p, sizes, tm_p, tp, True)
        o_a = _pool(a_in, zero_pool, w_bd, p_scale, bp, tp, 0)
        o_b = _diff_attn_prompt(qb, kb, vb, lam_vecs, sg, bp, tp, lam_init, tq, tk, 2)
        o_c = _stick_prompt(qc, kc, vc, bp, tp, tq)
        xp = tail(xp, o_a, o_b, o_c, gates, tm_p, tm_moe_p)
        rows_p.append((kb, vb, kc, vc, a_in.reshape(bp, tp, -1)[:, tp - POOL_STATE:]))

        a_in, qb, kb, vb, qc, kc, vc, gates = _in_proj(
            xs, g1, w_in_l, qg, kg, gm, tabs_s, pos_block_s, sizes, tm_s, ts, False)
        o_a = _pool(a_in, state_pool[l], w_bd, p_scale, bs, ts, past_len)
        o_b = _diff_decode(qb, kb, vb, ckb, cvb, page_table, l, lam_vecs, sg, ts, lam_init)
        o_c = _stick_decode(qc, kc, vc, ckc, cvc, page_table, l, ts)
        xs = tail(xs, o_a, o_b, o_c, gates, tm_s, tm_moe_s)
        new_pool = jnp.concatenate([state_pool[l], a_in.reshape(bs, ts, -1)], axis=1)[:, -POOL_STATE:]
        rows_s.append((kb, vb, kc, vc, new_pool))

    kb_p, vb_p, kc_p, vc_p, pool_p = [jnp.stack(z) for z in zip(*rows_p)]
    kb_s, vb_s, kc_s, vc_s, pool_s = [jnp.stack(z) for z in zip(*rows_s)]
    diff_shape = lambda b, t: (depth, b, t, diff_heads, 2 * HEAD_DIM)
    stick_t = lambda z: z.reshape(depth, bp, stick_heads, HEAD_DIM, tp).transpose(0, 1, 4, 2, 3)
    return (xp.reshape(bp, tp, d), xs.reshape(bs, ts, d),
            kb_p.reshape(diff_shape(bp, tp)), vb_p.reshape(diff_shape(bp, tp)), stick_t(kc_p), stick_t(vc_p), pool_p,
            kb_s.reshape(diff_shape(bs, ts)), vb_s.reshape(diff_shape(bs, ts)),
            kc_s.reshape(depth, bs, ts, stick_heads, HEAD_DIM), vc_s.reshape(depth, bs, ts, stick_heads, HEAD_DIM),
            pool_s)
```

```python
import functools
import math

import numpy as np
import jax
import jax.numpy as jnp
from jax import lax
from jax.experimental import pallas as pl
from jax.experimental.pallas import tpu as pltpu

F32 = jnp.float32
BF16 = jnp.bfloat16

HEAD_DIM = 64
ROT_DIM = HEAD_DIM // 4
ROPE_THETA = 500000.0
POOL_WINDOWS = (2, 4, 8, 16)
POOL_STATE = max(POOL_WINDOWS) - 1
N_GROUPS = 4
EXPERTS_PER_GROUP = 4
N_EXPERTS = N_GROUPS * EXPERTS_PER_GROUP
EPS = 1e-6
LANES = 128
SUBLANES = 8
ROUTE_OFF = N_GROUPS
VMEM_LIMIT = 56 * 1024 * 1024
NEG = -1e30
LOG2E = math.log2(math.e)


def _dot(a, b):
    return jnp.dot(a, b, preferred_element_type=F32)


def _dot_nt(a, b):
    return lax.dot_general(a, b, (((1,), (1,)), ((), ())), preferred_element_type=F32)


def _params(sem):
    return pltpu.CompilerParams(dimension_semantics=sem, vmem_limit_bytes=VMEM_LIMIT)


def _const_spec(shape):
    return pl.BlockSpec(shape, lambda *_: (0,) * len(shape), pipeline_mode=pl.Buffered(1))


def _in_proj_kernel(x_ref, g_ref, w_ref, qg_ref, kg_ref, gm_ref, c_ref, sa_ref, sb_ref, *rest,
                    sizes, cache_layout, n_prev):
    a_out, q_out, k_out, v_out, qc_out, kc_out, vc_out, gate_out = rest[n_prev:]
    x = x_ref[...]
    tm = x.shape[0]
    ms = jnp.mean(x * x, axis=-1, keepdims=True)
    h = ((x * lax.rsqrt(ms + EPS)) * g_ref[...]).astype(BF16)
    offs = np.concatenate([[0], np.cumsum(sizes)]).tolist()

    def seg(i, j=None):
        j = i if j is None else j
        return _dot(h, w_ref[:, offs[i]:offs[j + 1]])

    qk_w = sizes[1]
    reps = qk_w // LANES
    cos_t = jnp.tile(c_ref[...], (1, reps))
    sin_a = jnp.tile(sa_ref[...], (1, reps))
    sin_b = jnp.tile(sb_ref[...], (1, reps))
    half = ROT_DIM // 2

    def qk_norm_rope(y, gain):
        gms = _dot((y * y).astype(BF16), gm_ref[...])
        yn = (y * lax.rsqrt(gms + EPS)) * gain
        return (yn * cos_t + pltpu.roll(yn, qk_w - half, 1) * sin_a
                + pltpu.roll(yn, half, 1) * sin_b)

    def put_rows(ref, val):
        if cache_layout:
            for hd in range(reps):
                ref[pl.ds(hd, tm, stride=reps), :] = val[:, hd * LANES:(hd + 1) * LANES]
        else:
            ref[...] = val

    def put_cols(ref, val):
        ref[...] = val.T if cache_layout else val

    scale = HEAD_DIM ** -0.5
    a_out[...] = seg(0)
    q_out[...] = (qk_norm_rope(seg(1), qg_ref[...]) * (scale * LOG2E)).astype(q_out.dtype)
    put_rows(k_out, qk_norm_rope(seg(2), kg_ref[...]))
    put_rows(v_out, seg(3))
    qc_out[...] = (seg(4) * scale).astype(qc_out.dtype)
    put_cols(kc_out, seg(5))
    put_cols(vc_out, seg(6))
    gate_out[...] = jax.nn.sigmoid(seg(7, 9)).astype(gate_out.dtype)


def _in_proj(x, g1, w_in, qg, kg, gm, rope_tabs, pos_block, sizes, tm, seq_len, cache_layout,
             layer=0, depth=1, prev=()):
    n, d = x.shape
    cos_t, sin_a, sin_b = rope_tabs
    row = lambda i: (i, 0)
    tab = pl.BlockSpec((tm, LANES), lambda i: (pos_block(i), 0))
    diff_w, stick_w = sizes[1], sizes[4]
    heads = diff_w // LANES
    tiles = n // tm

    def plain(w, dt):
        return pl.BlockSpec((tm, w), row), jax.ShapeDtypeStruct((n, w), dt)

    if cache_layout:
        per_seq = seq_len // tm
        n_seq = n // seq_len
        rows = (pl.BlockSpec((tm * heads, LANES), lambda i: (layer * tiles + i, 0)),
                jax.ShapeDtypeStruct((depth * n * heads, LANES), F32))
        cols = (pl.BlockSpec((stick_w, tm), lambda i: (layer * n_seq + i // per_seq, i % per_seq)),
                jax.ShapeDtypeStruct((depth * n_seq * stick_w, seq_len), F32))
    else:
        rows, cols = plain(diff_w, F32), plain(stick_w, F32)
    outs = [plain(sizes[0], F32), plain(diff_w, BF16), rows, rows,
            plain(stick_w, BF16), cols, cols, plain(sum(sizes[7:]), BF16)]
    ins = [x, g1, w_in, qg, kg, gm, cos_t, sin_a, sin_b]
    cache_outs = (2, 3, 5, 6)
    return pl.pallas_call(
        functools.partial(_in_proj_kernel, sizes=tuple(sizes), cache_layout=cache_layout, n_prev=len(prev)),
        grid=(tiles,),
        in_specs=[pl.BlockSpec((tm, d), row), _const_spec((1, d)), _const_spec(w_in.shape),
                  _const_spec(qg.shape), _const_spec(kg.shape), _const_spec(gm.shape), tab, tab, tab]
        + [pl.BlockSpec(memory_space=pl.ANY)] * len(prev),
        out_specs=[o[0] for o in outs],
        out_shape=[o[1] for o in outs],
        input_output_aliases={len(ins) + j: cache_outs[j] for j in range(len(prev))},
        compiler_params=_params(("parallel",)),
        name="in_proj",
    )(*ins, *prev)


def _pool_kernel(a_ref, pre_ref, w_ref, s_ref, o_ref, u_scr, *, pos0, chunk):
    t, width = a_ref.shape
    pad = POOL_STATE + 1
    u_scr[0:1, :] = jnp.zeros((1, width), F32)
    u_scr[1:pad, :] = pre_ref[...]
    u_scr[pad:pad + t, :] = a_ref[...]
    group = width // len(POOL_WINDOWS)
    lane_g = lax.broadcasted_iota(jnp.int32, (1, width), 1) // group
    for c0 in range(0, t, chunk):
        a = u_scr[pad + c0:pad + c0 + chunk, :]
        pos = pos0 + c0 + lax.broadcasted_iota(jnp.int32, (chunk, 1), 0)
        acc = a
        pooled = jnp.zeros_like(a)
        for i in range(1, POOL_STATE + 1):
            acc = acc + u_scr[pad + c0 - i:pad + c0 - i + chunk, :]
            if i + 1 in POOL_WINDOWS:
                g = POOL_WINDOWS.index(i + 1)
                cnt = jnp.minimum(i + 1, pos + 1).astype(F32)
                pooled = jnp.where(lane_g == g, acc / cnt, pooled)
        pooled = pooled - a
        o_ref[c0:c0 + chunk, :] = (_dot(pooled.astype(BF16), w_ref[...]) * s_ref[...]).astype(o_ref.dtype)


def _pool(a_in, prefix, w_bd, scale, b, t, pos0):
    width = a_in.shape[1]
    chunk = min(t, 256)
    return pl.pallas_call(
        functools.partial(_pool_kernel, pos0=pos0, chunk=chunk),
        grid=(b,),
        in_specs=[pl.BlockSpec((t, width), lambda i: (i, 0)),
                  pl.BlockSpec((None, POOL_STATE, width), lambda i: (i, 0, 0)),
                  _const_spec(w_bd.shape), _const_spec(scale.shape)],
        out_specs=pl.BlockSpec((t, width), lambda i: (i, 0)),
        out_shape=jax.ShapeDtypeStruct(a_in.shape, BF16),
        scratch_shapes=[pltpu.VMEM((t + POOL_STATE + 1, width), F32)],
        compiler_params=_params(("parallel",)),
        name="pool",
    )(a_in, prefix, w_bd, scale)


def _lambda(lq1, lk1, lq2, lk2, lam_init):
    s1 = jnp.sum(lq1[...] * lk1[...], axis=-1, keepdims=True)
    s2 = jnp.sum(lq2[...] * lk2[...], axis=-1, keepdims=True)
    return jnp.exp(s1) - jnp.exp(s2) + lam_init


def _sub_norm(o, gain, lam_init):
    ms = jnp.mean(o * o, axis=-1, keepdims=True)
    return (o * lax.rsqrt(ms + EPS)) * gain * (1.0 - lam_init)


def _softplus(z):
    return jnp.maximum(z, 0.0) + jnp.log(1.0 + jnp.exp(-jnp.abs(z)))


def _split_bf16(x):
    hi = x.astype(BF16)
    lo = (x - hi.astype(F32)).astype(BF16)
    return hi, lo


def _pad_rows(x, rows):
    return jnp.concatenate([x, jnp.zeros((rows - x.shape[0], x.shape[1]), x.dtype)], axis=0)


def _lane_chunks(x):
    return [x[:, i * LANES:(i + 1) * LANES] for i in range(x.shape[1] // LANES)]


def _diff_attn_kernel(q_ref, k_ref, v_ref, lq1, lk1, lq2, lk2, sg_ref, o_ref, s_scr, *, tq, tk, heads, lam_init):
    hp = pl.program_id(1)
    qi = pl.program_id(2)
    hw = 2 * HEAD_DIM
    hpb = q_ref.shape[1] // hw
    lane = lax.broadcasted_iota(jnp.int32, (1, hw), 1)
    qs = []
    for j in range(hpb):
        q = q_ref[:, j * hw:(j + 1) * hw]
        zero = jnp.zeros_like(q)
        qs += [jnp.where(lane < HEAD_DIM, q, zero), jnp.where(lane >= HEAD_DIM, q, zero)]
    streams = range(2 * hpb)
    n_blk = (qi * tq + tq + tk - 1) // tk
    last = n_blk - 1

    def rows(ref, kb, j):
        start = pl.multiple_of(kb * (tk * heads), tk * heads) + hp * hpb + j
        return ref[pl.ds(start, tk, stride=heads), :].astype(BF16)

    def scores(kb, mx, masked):
        ks = [rows(k_ref, kb, j) for j in range(hpb)]
        col0 = pl.multiple_of(kb * tk, tk)
        s = [_dot_nt(qs[i], ks[i // 2]) for i in streams]
        if masked:
            row = qi * tq + lax.broadcasted_iota(jnp.int32, (tq, tk), 0)
            col = col0 + lax.broadcasted_iota(jnp.int32, (tq, tk), 1)
            s = [jnp.where(col <= row, s[i], NEG) for i in streams]
        for i in streams:
            s_scr[i, :, pl.ds(col0, tk)] = s[i]
        return tuple(functools.reduce(jnp.maximum, _lane_chunks(s[i]), mx[i]) for i in streams)

    neg = jnp.full((tq, LANES), NEG, F32)
    mx = lax.fori_loop(0, last, lambda kb, m: scores(kb, m, False), (neg,) * len(streams))
    mx = scores(last, mx, True)
    m_b = [jnp.broadcast_to(jnp.max(m, axis=-1, keepdims=True), (tq, LANES)) for m in mx]

    def accumulate(kb, carry):
        vs = [rows(v_ref, kb, j) for j in range(hpb)]
        col0 = pl.multiple_of(kb * tk, tk)
        p = [[jnp.exp2(sc - m_b[i]) for sc in _lane_chunks(s_scr[i, :, pl.ds(col0, tk)])] for i in streams]
        den = [functools.reduce(jnp.add, p[i], carry[i][0]) for i in streams]
        acc = [carry[i][1] + _dot(jnp.concatenate(p[i], axis=1).astype(BF16), vs[i // 2]) for i in streams]
        return tuple((den[i], acc[i]) for i in streams)

    zero_c = (jnp.zeros((tq, LANES), F32), jnp.zeros((tq, hw), F32))
    res = lax.fori_loop(0, n_blk, accumulate, (zero_c,) * len(streams))
    lam = _lambda(lq1, lk1, lq2, lk2, lam_init)
    norm = [acc / jnp.sum(den, axis=-1, keepdims=True) for den, acc in res]
    for j in range(hpb):
        o = norm[2 * j] - lam * norm[2 * j + 1]
        o_ref[:, j * hw:(j + 1) * hw] = _sub_norm(o, sg_ref[...], lam_init).astype(o_ref.dtype)


def _diff_attn_prompt(q, k, v, lam_vecs, sg, b, t, lam_init, tq, tk, hpb, layer):
    n, w = q.shape
    hw = 2 * HEAD_DIM
    heads = w // hw
    nq = t // tq
    vec = _const_spec((1, HEAD_DIM))
    kv = pl.BlockSpec((t * heads, hw), lambda bi, h, i: (layer * b + bi, 0))
    return pl.pallas_call(
        functools.partial(_diff_attn_kernel, tq=tq, tk=tk, heads=heads, lam_init=lam_init),
        grid=(b, heads // hpb, nq),
        in_specs=[pl.BlockSpec((tq, hpb * hw), lambda bi, h, i: (bi * nq + i, h)), kv, kv,
                  vec, vec, vec, vec, _const_spec((1, hw))],
        out_specs=pl.BlockSpec((tq, hpb * hw), lambda bi, h, i: (bi * nq + i, h)),
        out_shape=jax.ShapeDtypeStruct((n, w), BF16),
        scratch_shapes=[pltpu.VMEM((2 * hpb, tq, t), F32)],
        compiler_params=_params(("parallel", "arbitrary", "arbitrary")),
        name="diff_attn_prompt",
    )(q, k, v, *lam_vecs, sg)


def _stick_kernel(q_ref, k_ref, v_ref, o_ref, *, tq):
    qi = pl.program_id(1)
    q = q_ref[...]
    w = q.shape[1]
    lane_h = lax.broadcasted_iota(jnp.int32, (1, w), 1) // HEAD_DIM
    row = lax.broadcasted_iota(jnp.int32, (tq, tq), 0)
    col = lax.broadcasted_iota(jnp.int32, (tq, tq), 1)
    tri = jnp.where(row > col, 1.0, 0.0).astype(BF16)
    heads = w // HEAD_DIM
    qh = [jnp.where(lane_h == hh, q, jnp.zeros_like(q)) for hh in range(heads)]

    def step(kb, carry, diag):
        start = pl.multiple_of(kb * tq, tq)
        k = k_ref[:, pl.ds(start, tq)].astype(BF16)
        v = v_ref[:, pl.ds(start, tq)].astype(BF16)
        hs = range(heads)
        z = [_dot(qh[hh], k) for hh in hs]
        sp = [_softplus(z[hh]) for hh in hs]
        lk = [-sp[hh] for hh in hs]
        if diag:
            lk = [jnp.where(col < row, lk[hh], 0.0) for hh in hs]
        parts = [_split_bf16(lk[hh]) for hh in hs]
        later = [_dot(parts[hh][0], tri) + _dot(parts[hh][1], tri) + carry[hh][1] for hh in hs]
        a = [jnp.exp(z[hh] - sp[hh] + later[hh]) for hh in hs]
        if diag:
            a = [jnp.where(col < row, a[hh], 0.0) for hh in hs]
        acc = [carry[hh][0] + _dot_nt(a[hh].astype(BF16), v[hh * HEAD_DIM:(hh + 1) * HEAD_DIM, :]) for hh in hs]
        run = [carry[hh][1] + jnp.sum(lk[hh], axis=-1, keepdims=True) for hh in hs]
        return tuple((acc[hh], run[hh]) for hh in hs)

    init = tuple((jnp.zeros((tq, HEAD_DIM), F32), jnp.zeros((tq, 1), F32)) for _ in range(heads))
    carry = step(qi, init, True)
    carry = lax.fori_loop(0, qi, lambda j, c: step(qi - 1 - j, c, False), carry)
    o_ref[...] = jnp.concatenate([c[0] for c in carry], axis=1).astype(o_ref.dtype)


def _stick_prompt(q, k_t, v_t, b, t, tq, layer):
    n, w = q.shape
    nq = t // tq
    kv = pl.BlockSpec((w, t), lambda bi, i: (layer * b + bi, 0))
    return pl.pallas_call(
        functools.partial(_stick_kernel, tq=tq),
        grid=(b, nq),
        in_specs=[pl.BlockSpec((tq, w), lambda bi, i: (bi * nq + i, 0)), kv, kv],
        out_specs=pl.BlockSpec((tq, w), lambda bi, i: (bi * nq + i, 0)),
        out_shape=jax.ShapeDtypeStruct((n, w), BF16),
        compiler_params=_params(("parallel", "arbitrary")),
        name="stick_prompt",
    )(q, k_t, v_t)


def _head_rows(x, heads):
    return jnp.concatenate([x[:, h * LANES:(h + 1) * LANES] for h in range(heads)], axis=0)


def _diff_decode_kernel(pt_ref, q_ref, kn_ref, vn_ref, *rest, n_pages, lam_init):
    del pt_ref
    kp = rest[:n_pages]
    vp = rest[n_pages:2 * n_pages]
    lq1, lk1, lq2, lk2, sg_ref, o_ref, s_scr = rest[2 * n_pages:]
    t, w = q_ref.shape
    hw = 2 * HEAD_DIM
    heads = w // hw
    prow = kp[0].shape[0]
    nq = 2 * heads * t
    q = q_ref[...].astype(F32)
    lane = lax.broadcasted_iota(jnp.int32, (1, hw), 1)
    pieces = []
    for h in range(heads):
        qh = q[:, h * hw:(h + 1) * hw]
        pieces += [jnp.where(lane < HEAD_DIM, qh, 0.0), jnp.where(lane >= HEAD_DIM, qh, 0.0)]
    qrows = jnp.concatenate(pieces, axis=0).astype(BF16)
    row_h = lax.broadcasted_iota(jnp.int32, (nq, prow), 0) // (2 * t)
    col_h = lax.broadcasted_iota(jnp.int32, (nq, prow), 1) % heads
    page_mask = row_h == col_h
    for j in range(n_pages):
        s = _dot_nt(qrows, kp[j][...].astype(BF16))
        s_scr[:, j * prow:(j + 1) * prow] = jnp.where(page_mask, s, NEG)
    k_new = _pad_rows(_head_rows(kn_ref[...], heads), LANES).astype(BF16)
    v_new = _pad_rows(_head_rows(vn_ref[...], heads), LANES).astype(BF16)
    r = lax.broadcasted_iota(jnp.int32, (nq, LANES), 0)
    c = lax.broadcasted_iota(jnp.int32, (nq, LANES), 1)
    new_mask = (c < heads * t) & (r // (2 * t) == c // t) & (c % t <= r % t)
    s_scr[:, n_pages * prow:] = jnp.where(new_mask, _dot_nt(qrows, k_new), NEG)
    m = jnp.max(s_scr[...], axis=-1, keepdims=True)
    acc = jnp.zeros((nq, hw), F32)
    den = jnp.zeros((nq, 1), F32)
    for j in range(n_pages + 1):
        width = prow if j < n_pages else LANES
        p = jnp.exp2(s_scr[:, j * prow:j * prow + width] - m)
        den = den + jnp.sum(p, axis=-1, keepdims=True)
        v = vp[j][...].astype(BF16) if j < n_pages else v_new
        acc = acc + _dot(p.astype(BF16), v)
    o = acc / den
    lam = _lambda(lq1, lk1, lq2, lk2, lam_init)
    for h in range(heads):
        r0 = 2 * h * t
        oh = o[r0:r0 + t] - lam * o[r0 + t:r0 + 2 * t]
        o_ref[:, h * hw:(h + 1) * hw] = _sub_norm(oh, sg_ref[...], lam_init).astype(o_ref.dtype)


def _page_specs(cache, layer, n_pages):
    blk = (None, None) + cache.shape[2:]
    return [pl.BlockSpec(blk, lambda b, pt, j=j: (layer, pt[b, j], 0, 0)) for j in range(n_pages)]


def _diff_decode(q, k_new, v_new, cache_k, cache_v, page_table, layer, lam_vecs, sg, t, lam_init):
    n, w = q.shape
    b, n_pages = page_table.shape
    prow = cache_k.shape[2]
    heads = w // (2 * HEAD_DIM)
    tok = pl.BlockSpec((t, w), lambda bi, pt: (bi, 0))
    vec = pl.BlockSpec((1, HEAD_DIM), lambda bi, pt: (0, 0))
    grid_spec = pltpu.PrefetchScalarGridSpec(
        num_scalar_prefetch=1,
        grid=(b,),
        in_specs=[tok, tok, tok] + _page_specs(cache_k, layer, n_pages) + _page_specs(cache_v, layer, n_pages)
        + [vec, vec, vec, vec, pl.BlockSpec((1, 2 * HEAD_DIM), lambda bi, pt: (0, 0))],
        out_specs=tok,
        scratch_shapes=[pltpu.VMEM((2 * heads * t, n_pages * prow + LANES), F32)],
    )
    return pl.pallas_call(
        functools.partial(_diff_decode_kernel, n_pages=n_pages, lam_init=lam_init),
        grid_spec=grid_spec,
        out_shape=jax.ShapeDtypeStruct((n, w), BF16),
        compiler_params=_params(("parallel",)),
        name="diff_decode",
    )(page_table, q, k_new, v_new, *([cache_k] * n_pages), *([cache_v] * n_pages), *lam_vecs, sg)


def _stick_decode_kernel(pt_ref, q_ref, kn_ref, vn_ref, *rest, n_pages):
    del pt_ref
    kp = rest[:n_pages]
    vp = rest[n_pages:2 * n_pages]
    o_ref = rest[2 * n_pages]
    t, w = q_ref.shape
    heads = w // HEAD_DIM
    page = kp[0].shape[1]
    nq = heads * t
    q = q_ref[...].astype(F32)
    lane_h = lax.broadcasted_iota(jnp.int32, (1, w), 1) // HEAD_DIM
    qrows = jnp.concatenate([jnp.where(lane_h == h, q, 0.0) for h in range(heads)], axis=0).astype(BF16)
    r = lax.broadcasted_iota(jnp.int32, (page, page), 0)
    c = lax.broadcasted_iota(jnp.int32, (page, page), 1)
    tri = jnp.where(r > c, 1.0, 0.0).astype(BF16)
    qr = lax.broadcasted_iota(jnp.int32, (nq, page), 0) % t
    kc = lax.broadcasted_iota(jnp.int32, (nq, page), 1)
    new_mask = kc < qr
    pages = range(n_pages + 1)
    z = [_dot(qrows, kp[j][...].astype(BF16)) for j in range(n_pages)]
    z.append(_dot_nt(qrows, _pad_rows(kn_ref[...], page).astype(BF16)))
    sp = [_softplus(z[j]) for j in pages]
    lk = [-sp[j] for j in pages]
    lk[n_pages] = jnp.where(new_mask, lk[n_pages], 0.0)
    parts = [_split_bf16(lk[j]) for j in pages]
    within = [_dot(parts[j][0], tri) + _dot(parts[j][1], tri) for j in pages]
    totals = [jnp.sum(lk[j], axis=-1, keepdims=True) for j in pages]
    run = jnp.zeros((nq, 1), F32)
    a = [None] * (n_pages + 1)
    for j in range(n_pages, -1, -1):
        a[j] = jnp.exp(z[j] - sp[j] + within[j] + run)
        run = run + totals[j]
    a[n_pages] = jnp.where(new_mask, a[n_pages], 0.0)
    acc = _dot(a[n_pages].astype(BF16), _pad_rows(vn_ref[...], page).astype(BF16))
    for j in range(n_pages):
        acc = acc + _dot_nt(a[j].astype(BF16), vp[j][...].astype(BF16))
    o = jnp.zeros((t, w), F32)
    for h in range(heads):
        o = jnp.where(lane_h == h, acc[h * t:(h + 1) * t], o)
    o_ref[...] = o.astype(o_ref.dtype)


def _stick_decode(q, k_new, v_new, cache_k, cache_v, page_table, layer, t):
    n, w = q.shape
    b, n_pages = page_table.shape
    tok = pl.BlockSpec((t, w), lambda bi, pt: (bi, 0))
    grid_spec = pltpu.PrefetchScalarGridSpec(
        num_scalar_prefetch=1,
        grid=(b,),
        in_specs=[tok, tok, tok] + _page_specs(cache_k, layer, n_pages) + _page_specs(cache_v, layer, n_pages),
        out_specs=tok,
    )
    return pl.pallas_call(
        functools.partial(_stick_decode_kernel, n_pages=n_pages),
        grid_spec=grid_spec,
        out_shape=jax.ShapeDtypeStruct((n, w), BF16),
        compiler_params=_params(("parallel",)),
        name="stick_decode",
    )(page_table, q, k_new, v_new, *([cache_k] * n_pages), *([cache_v] * n_pages))


def _merge_kernel(x_ref, oa_ref, ob_ref, oc_ref, gate_ref, wa_ref, wb_ref, wc_ref, wo_ref,
                  g2_ref, wr_hi_ref, wr_lo_ref, br_ref, x_out, h_out, comb_out):
    d = x_ref.shape[1]
    gate = gate_ref[...].astype(F32)
    merged = (gate[:, 0:d] * _dot(oa_ref[...], wa_ref[...])
              + gate[:, d:2 * d] * _dot(ob_ref[...], wb_ref[...])
              + gate[:, 2 * d:3 * d] * _dot(oc_ref[...], wc_ref[...]))
    x = x_ref[...] + _dot(merged.astype(BF16), wo_ref[...])
    x_out[...] = x
    ms = jnp.mean(x * x, axis=-1, keepdims=True)
    h = (x * lax.rsqrt(ms + EPS)) * g2_ref[...]
    h_out[...] = h.astype(h_out.dtype)
    h_hi, h_lo = _split_bf16(h)
    logits = (_dot(h_hi, wr_hi_ref[...]) + _dot(h_lo, wr_hi_ref[...]) + _dot(h_hi, wr_lo_ref[...])
              + br_ref[...])
    lane = lax.broadcasted_iota(jnp.int32, logits.shape, 1).astype(F32)
    gl = jnp.where(lane < N_GROUPS, logits, NEG)
    gmax = jnp.max(gl, axis=-1, keepdims=True)
    gidx = jnp.min(jnp.where(gl == gmax, lane, 1e9), axis=-1, keepdims=True)
    gprob = 1.0 / jnp.sum(jnp.where(lane < N_GROUPS, jnp.exp(logits - gmax), 0.0), axis=-1, keepdims=True)
    rel = lane - ROUTE_OFF - EXPERTS_PER_GROUP * gidx
    el = jnp.where((rel >= 0) & (rel < EXPERTS_PER_GROUP), logits, NEG)
    t1 = jnp.max(el, axis=-1, keepdims=True)
    i1 = jnp.min(jnp.where(el == t1, lane, 1e9), axis=-1, keepdims=True)
    el2 = jnp.where(lane == i1, NEG, el)
    t2 = jnp.max(el2, axis=-1, keepdims=True)
    i2 = jnp.min(jnp.where(el2 == t2, lane, 1e9), axis=-1, keepdims=True)
    e21 = jnp.exp(t2 - t1)
    w1 = gprob / (1.0 + e21)
    w2 = w1 * e21
    comb_out[...] = jnp.where(lane == 0, gidx, jnp.where(lane == i1, w1, jnp.where(lane == i2, w2, 0.0)))


def _merge(x, o_a, o_b, o_c, gates, wa, wb, wc, wo, g2, wr_hi, wr_lo, br, tm):
    n, d = x.shape
    row = lambda i: (i, 0)
    ins = [x, o_a, o_b, o_c, gates]
    consts = [wa, wb, wc, wo, g2, wr_hi, wr_lo, br]
    return pl.pallas_call(
        _merge_kernel,
        grid=(n // tm,),
        in_specs=[pl.BlockSpec((tm, a.shape[1]), row) for a in ins] + [_const_spec(c.shape) for c in consts],
        out_specs=[pl.BlockSpec((tm, d), row), pl.BlockSpec((tm, d), row), pl.BlockSpec((tm, LANES), row)],
        out_shape=[jax.ShapeDtypeStruct((n, d), F32), jax.ShapeDtypeStruct((n, d), BF16),
                   jax.ShapeDtypeStruct((n, LANES), F32)],
        compiler_params=_params(("parallel",)),
        name="merge",
    )(*ins, *consts)


def _dispatch_plan(comb, tm):
    n = comb.shape[0]
    nt = n // tm
    g = comb[:, 0].astype(jnp.int32).reshape(nt, tm)
    onehot = (g[:, :, None] == jnp.arange(N_GROUPS, dtype=jnp.int32)[None, None, :]).astype(jnp.int32)
    rank = jnp.cumsum(onehot, axis=1) - onehot
    counts = jnp.sum(onehot, axis=1)
    padded = (counts + SUBLANES - 1) // SUBLANES * SUBLANES
    gstart = jnp.cumsum(padded, axis=1) - padded
    dest = jnp.sum((gstart[:, None, :] + rank) * onehot, axis=2)
    return dest.reshape(n), gstart.reshape(-1), counts.reshape(-1)


def _moe_kernel(dest_ref, gstart_ref, count_ref, x_ref, h_ref, comb_ref, wgu_ref, wd_ref, o_ref,
                hf, hs, cs, ys, *, tm, win):
    i = pl.program_id(0)
    g = pl.program_id(1)
    base = i * tm
    f_all = wd_ref.shape[0]
    f = f_all // EXPERTS_PER_GROUP

    @pl.when(g == 0)
    def _sort():
        hf[...] = h_ref[...].astype(F32)
        hs[...] = jnp.zeros_like(hs)
        cs[...] = jnp.zeros_like(cs)
        ys[...] = jnp.zeros_like(ys)

        def body(r, carry):
            d = dest_ref[base + r]
            hs[pl.ds(d, 1), :] = hf[pl.ds(r, 1), :]
            cs[pl.ds(d, 1), :] = comb_ref[pl.ds(r, 1), :]
            return carry

        lax.fori_loop(0, tm, body, 0, unroll=8)

    gs = gstart_ref[i * N_GROUPS + g]
    n_win = (count_ref[i * N_GROUPS + g] + win - 1) // win
    lane = lax.broadcasted_iota(jnp.int32, (1, LANES), 1)

    def window(w, carry):
        r0 = pl.multiple_of(gs + w * win, SUBLANES)
        hw = hs[pl.ds(r0, win), :].astype(BF16)
        cw = cs[pl.ds(r0, win), :]
        gu = _dot(hw, wgu_ref[...])
        ce = [jnp.sum(jnp.where(lane == ROUTE_OFF + EXPERTS_PER_GROUP * g + e, cw, 0.0), axis=-1, keepdims=True)
              for e in range(EXPERTS_PER_GROUP)]
        ce_all = jnp.concatenate([jnp.broadcast_to(c, (win, f)) for c in ce], axis=1)
        act = jax.nn.silu(gu[:, :f_all]) * gu[:, f_all:] * ce_all
        ys[pl.ds(r0, win), :] += _dot(act.astype(BF16), wd_ref[...])
        return carry

    lax.fori_loop(0, n_win, window, 0)

    @pl.when(g == pl.num_programs(1) - 1)
    def _unsort():
        def body(r, carry):
            d = dest_ref[base + r]
            hf[pl.ds(r, 1), :] = ys[pl.ds(d, 1), :]
            return carry

        lax.fori_loop(0, tm, body, 0, unroll=8)
        o_ref[...] = x_ref[...] + hf[...]


def _moe(x, h, comb, w_gu, w_d, tm):
    n, d = x.shape
    win = tm // N_GROUPS + 48
    rows = tm + N_GROUPS * SUBLANES + win
    dest, gstart, counts = _dispatch_plan(comb, tm)
    once = pl.Buffered(1)
    row = lambda i, g, *_: (i, 0)
    grid_spec = pltpu.PrefetchScalarGridSpec(
        num_scalar_prefetch=3,
        grid=(n // tm, N_GROUPS),
        in_specs=[pl.BlockSpec((tm, d), row, pipeline_mode=once), pl.BlockSpec((tm, d), row, pipeline_mode=once),
                  pl.BlockSpec((tm, LANES), row, pipeline_mode=once),
                  pl.BlockSpec((None,) + w_gu.shape[1:], lambda i, g, *_: (g, 0, 0)),
                  pl.BlockSpec((None,) + w_d.shape[1:], lambda i, g, *_: (g, 0, 0))],
        out_specs=pl.BlockSpec((tm, d), row, pipeline_mode=once),
        scratch_shapes=[pltpu.VMEM((tm, d), F32), pltpu.VMEM((rows, d), F32),
                        pltpu.VMEM((rows, LANES), F32), pltpu.VMEM((rows, d), F32)],
    )
    return pl.pallas_call(
        functools.partial(_moe_kernel, tm=tm, win=win),
        grid_spec=grid_spec,
        out_shape=jax.ShapeDtypeStruct((n, d), F32),
        compiler_params=_params(("parallel", "arbitrary")),
        name="moe",
    )(dest, gstart, counts, x, h, comb, w_gu, w_d)


def _rope_tables(positions):
    half = ROT_DIM // 2
    inv = ROPE_THETA ** (-jnp.arange(0, ROT_DIM, 2, dtype=F32) / ROT_DIM)
    ang = positions.astype(F32)[:, None] * inv[None, :]
    cos, sin = jnp.cos(ang), jnp.sin(ang)
    n = positions.shape[0]
    rest = HEAD_DIM - ROT_DIM
    comp_c = jnp.concatenate([cos, cos, jnp.ones((n, rest), F32)], axis=1)
    comp_a = jnp.concatenate([-sin, jnp.zeros((n, half + rest), F32)], axis=1)
    comp_b = jnp.concatenate([jnp.zeros((n, half), F32), sin, jnp.zeros((n, rest), F32)], axis=1)
    return tuple(jnp.tile(c, (1, LANES // HEAD_DIM)) for c in (comp_c, comp_a, comp_b))


def _block_diag(blocks):
    g, a, b = blocks.shape
    out = jnp.zeros((g * a, g * b), blocks.dtype)
    for i in range(g):
        out = out.at[i * a:(i + 1) * a, i * b:(i + 1) * b].set(blocks[i])
    return out


def kernel(x_prompt, x_sample, cache_kb, cache_vb, cache_kc, cache_vc, state_pool, page_table, norm1_g, w_in, pool_w, pool_scale, qn_g, kn_g, lam_q1, lam_k1, lam_q2, lam_k2, subln_g, w_proj_a, w_proj_b, w_proj_c, w_out, norm2_g, w_group, b_group, w_router, b_router, w_e_gate, w_e_up, w_e_down):
    bp, tp, d = x_prompt.shape
    bs, ts, _ = x_sample.shape
    depth = w_in.shape[0]
    n_pages, page = page_table.shape[1], cache_kb.shape[2]
    past_len = n_pages * page
    pool_width = pool_w.shape[1] * pool_w.shape[2]
    diff_heads, stick_heads = cache_kb.shape[3], cache_kc.shape[3]
    diff_w = diff_heads * 2 * HEAD_DIM
    stick_w = stick_heads * HEAD_DIM
    sizes = (pool_width, diff_w, diff_w, diff_w, stick_w, stick_w, stick_w, d, d, d)

    tm_p = min(512, tp)
    tm_s = min(512, bs * ts)
    tq = min(256, tp)
    tk = min(512, tp)
    tm_moe_p = min(1024, bp * tp)
    tm_moe_s = min(1024, bs * ts)
    tabs_p = _rope_tables(jnp.arange(tp))
    tabs_s = _rope_tables(past_len + (jnp.arange(tm_s) % ts))
    pos_block_p = lambda i: i % (tp // tm_p)
    pos_block_s = lambda i: 0

    gm = _block_diag(jnp.full((diff_w // HEAD_DIM, HEAD_DIM, HEAD_DIM), 1.0 / HEAD_DIM, F32)).astype(BF16)
    n_pool_pages = cache_kb.shape[1]
    ckb, cvb = (c.reshape(depth, n_pool_pages, page * diff_heads, 2 * HEAD_DIM) for c in (cache_kb, cache_vb))
    ckc, cvc = (c.transpose(0, 1, 3, 4, 2).reshape(depth, n_pool_pages, stick_w, page) for c in (cache_kc, cache_vc))
    zero_pool = jnp.zeros((bp, POOL_STATE, pool_width), F32)

    xp = x_prompt.reshape(bp * tp, d)
    xs = x_sample.reshape(bs * ts, d)
    pool_rows_p, rows_s = [], []
    cache_p = ()
    for l in range(depth):
        lam_init = 0.8 - 0.6 * math.exp(-0.3 * l)
        g1 = norm1_g[l].reshape(1, d)
        w_in_l = w_in[l].astype(BF16)
        qg = jnp.tile(qn_g[l].reshape(1, -1), (1, diff_heads))
        kg = jnp.tile(kn_g[l].reshape(1, -1), (1, diff_heads))
        w_bd = _block_diag(pool_w[l]).astype(BF16)
        p_scale = pool_scale[l].reshape(1, -1)
        lam_vecs = tuple(v[l].reshape(1, -1) for v in (lam_q1, lam_k1, lam_q2, lam_k2))
        sg = subln_g[l].reshape(1, -1)
        wa, wb, wc, wo = (w[l].astype(BF16) for w in (w_proj_a, w_proj_b, w_proj_c, w_out))
        g2 = norm2_g[l].reshape(1, d)
        wr = jnp.concatenate([w_group[l], w_router[l],
                              jnp.zeros((d, LANES - N_GROUPS - N_EXPERTS), F32)], axis=1)
        wr_hi = wr.astype(BF16)
        wr_lo = (wr - wr_hi.astype(F32)).astype(BF16)
        br = jnp.concatenate([b_group[l], b_router[l],
                              jnp.zeros((LANES - N_GROUPS - N_EXPERTS,), F32)]).reshape(1, LANES)
        d_exp = w_e_gate.shape[3]
        by_group = lambda w: (w[l].reshape(N_GROUPS, EXPERTS_PER_GROUP, d, d_exp).transpose(0, 2, 1, 3)
                              .reshape(N_GROUPS, d, EXPERTS_PER_GROUP * d_exp))
        w_gu = jnp.concatenate([by_group(w_e_gate), by_group(w_e_up)], axis=2).astype(BF16)
        w_d = w_e_down[l].reshape(N_GROUPS, EXPERTS_PER_GROUP * d_exp, d).astype(BF16)

        def tail(x, o_a, o_b, o_c, gates, tm, tm_moe):
            x_mid, h2, comb = _merge(x, o_a, o_b, o_c, gates, wa, wb, wc, wo, g2, wr_hi, wr_lo, br, tm)
            return _moe(x_mid, h2, comb, w_gu, w_d, tm_moe)

        a_in, qb, kb, vb, qc, kc, vc, gates = _in_proj(
            xp, g1, w_in_l, qg, kg, gm, tabs_p, pos_block_p, sizes, tm_p, tp, True, l, depth, cache_p)
        cache_p = (kb, vb, kc, vc)
        o_a = _pool(a_in, zero_pool, w_bd, p_scale, bp, tp, 0)
        o_b = _diff_attn_prompt(qb, kb, vb, lam_vecs, sg, bp, tp, lam_init, tq, tk, 2, l)
        o_c = _stick_prompt(qc, kc, vc, bp, tp, tq, l)
        xp = tail(xp, o_a, o_b, o_c, gates, tm_p, tm_moe_p)
        pool_rows_p.append(a_in.reshape(bp, tp, -1)[:, tp - POOL_STATE:])

        a_in, qb, kb, vb, qc, kc, vc, gates = _in_proj(
            xs, g1, w_in_l, qg, kg, gm, tabs_s, pos_block_s, sizes, tm_s, ts, False)
        o_a = _pool(a_in, state_pool[l], w_bd, p_scale, bs, ts, past_len)
        o_b = _diff_decode(qb, kb, vb, ckb, cvb, page_table, l, lam_vecs, sg, ts, lam_init)
        o_c = _stick_decode(qc, kc, vc, ckc, cvc, page_table, l, ts)
        xs = tail(xs, o_a, o_b, o_c, gates, tm_s, tm_moe_s)
        new_pool = jnp.concatenate([state_pool[l], a_in.reshape(bs, ts, -1)], axis=1)[:, -POOL_STATE:]
        rows_s.append((kb, vb, kc, vc, new_pool))

    kb_p, vb_p, kc_p, vc_p = cache_p
    pool_p = jnp.stack(pool_rows_p)
    kb_s, vb_s, kc_s, vc_s, pool_s = [jnp.stack(z) for z in zip(*rows_s)]
    diff_shape = lambda b, t: (depth, b, t, diff_heads, 2 * HEAD_DIM)
    stick_t = lambda z: z.reshape(depth, bp, stick_heads, HEAD_DIM, tp).transpose(0, 1, 4, 2, 3)
    return (xp.reshape(bp, tp, d), xs.reshape(bs, ts, d),
            kb_p.reshape(diff_shape(bp, tp)), vb_p.reshape(diff_shape(bp, tp)), stick_t(kc_p), stick_t(vc_p), pool_p,
            kb_s.reshape(diff_shape(bs, ts)), vb_s.reshape(diff_shape(bs, ts)),
            kc_s.reshape(depth, bs, ts, stick_heads, HEAD_DIM), vc_s.reshape(depth, bs, ts, stick_heads, HEAD_DIM),
            pool_s)
```

```python
import functools
import math

import numpy as np
import jax
import jax.numpy as jnp
from jax import lax
from jax.experimental import pallas as pl
from jax.experimental.pallas import tpu as pltpu

F32 = jnp.float32
BF16 = jnp.bfloat16

HEAD_DIM = 64
ROT_DIM = HEAD_DIM // 4
ROPE_THETA = 500000.0
POOL_WINDOWS = (2, 4, 8, 16)
POOL_STATE = max(POOL_WINDOWS) - 1
N_GROUPS = 4
EXPERTS_PER_GROUP = 4
N_EXPERTS = N_GROUPS * EXPERTS_PER_GROUP
EPS = 1e-6
LANES = 128
SUBLANES = 8
ROUTE_OFF = N_GROUPS
VMEM_LIMIT = 56 * 1024 * 1024
NEG = -1e30
LOG2E = math.log2(math.e)


def _dot(a, b):
    return jnp.dot(a, b, preferred_element_type=F32)


def _dot_nt(a, b):
    return lax.dot_general(a, b, (((1,), (1,)), ((), ())), preferred_element_type=F32)


def _params(sem):
    return pltpu.CompilerParams(dimension_semantics=sem, vmem_limit_bytes=VMEM_LIMIT)


def _const_spec(shape):
    return pl.BlockSpec(shape, lambda *_: (0,) * len(shape), pipeline_mode=pl.Buffered(1))


def _in_proj_kernel(x_ref, g_ref, w_ref, qg_ref, kg_ref, gm_ref, c_ref, sa_ref, sb_ref, *rest,
                    sizes, cache_layout, n_prev):
    a_out, q_out, k_out, v_out, qc_out, kc_out, vc_out, gate_out = rest[n_prev:]
    x = x_ref[...]
    tm = x.shape[0]
    ms = jnp.mean(x * x, axis=-1, keepdims=True)
    h = ((x * lax.rsqrt(ms + EPS)) * g_ref[...]).astype(BF16)
    offs = np.concatenate([[0], np.cumsum(sizes)]).tolist()

    def seg(i, j=None):
        j = i if j is None else j
        return _dot(h, w_ref[:, offs[i]:offs[j + 1]])

    qk_w = sizes[1]
    reps = qk_w // LANES
    cos_t = jnp.tile(c_ref[...], (1, reps))
    sin_a = jnp.tile(sa_ref[...], (1, reps))
    sin_b = jnp.tile(sb_ref[...], (1, reps))
    half = ROT_DIM // 2

    def qk_norm_rope(y, gain):
        gms = _dot((y * y).astype(BF16), gm_ref[...])
        yn = (y * lax.rsqrt(gms + EPS)) * gain
        return (yn * cos_t + pltpu.roll(yn, qk_w - half, 1) * sin_a
                + pltpu.roll(yn, half, 1) * sin_b)

    def put_rows(ref, val):
        if cache_layout:
            for hd in range(reps):
                ref[pl.ds(hd, tm, stride=reps), :] = val[:, hd * LANES:(hd + 1) * LANES]
        else:
            ref[...] = val

    def put_cols(ref, val):
        ref[...] = val.T if cache_layout else val

    scale = HEAD_DIM ** -0.5
    a_out[...] = seg(0)
    q_out[...] = (qk_norm_rope(seg(1), qg_ref[...]) * (scale * LOG2E)).astype(q_out.dtype)
    put_rows(k_out, qk_norm_rope(seg(2), kg_ref[...]))
    put_rows(v_out, seg(3))
    qc_out[...] = (seg(4) * scale).astype(qc_out.dtype)
    put_cols(kc_out, seg(5))
    put_cols(vc_out, seg(6))
    gate_out[...] = jax.nn.sigmoid(seg(7, 9)).astype(gate_out.dtype)


def _in_proj(x, g1, w_in, qg, kg, gm, rope_tabs, pos_block, sizes, tm, seq_len, cache_layout,
             layer=0, depth=1, prev=()):
    n, d = x.shape
    cos_t, sin_a, sin_b = rope_tabs
    row = lambda i: (i, 0)
    tab = pl.BlockSpec((tm, LANES), lambda i: (pos_block(i), 0))
    diff_w, stick_w = sizes[1], sizes[4]
    heads = diff_w // LANES
    tiles = n // tm

    def plain(w, dt):
        return pl.BlockSpec((tm, w), row), jax.ShapeDtypeStruct((n, w), dt)

    if cache_layout:
        per_seq = seq_len // tm
        n_seq = n // seq_len
        rows = (pl.BlockSpec((tm * heads, LANES), lambda i: (layer * tiles + i, 0)),
                jax.ShapeDtypeStruct((depth * n * heads, LANES), F32))
        cols = (pl.BlockSpec((stick_w, tm), lambda i: (layer * n_seq + i // per_seq, i % per_seq)),
                jax.ShapeDtypeStruct((depth * n_seq * stick_w, seq_len), F32))
    else:
        rows, cols = plain(diff_w, F32), plain(stick_w, F32)
    outs = [plain(sizes[0], F32), plain(diff_w, BF16), rows, rows,
            plain(stick_w, BF16), cols, cols, plain(sum(sizes[7:]), BF16)]
    ins = [x, g1, w_in, qg, kg, gm, cos_t, sin_a, sin_b]
    cache_outs = (2, 3, 5, 6)
    return pl.pallas_call(
        functools.partial(_in_proj_kernel, sizes=tuple(sizes), cache_layout=cache_layout, n_prev=len(prev)),
        grid=(tiles,),
        in_specs=[pl.BlockSpec((tm, d), row), _const_spec((1, d)), _const_spec(w_in.shape),
                  _const_spec(qg.shape), _const_spec(kg.shape), _const_spec(gm.shape), tab, tab, tab]
        + [pl.BlockSpec(memory_space=pl.ANY)] * len(prev),
        out_specs=[o[0] for o in outs],
        out_shape=[o[1] for o in outs],
        input_output_aliases={len(ins) + j: cache_outs[j] for j in range(len(prev))},
        compiler_params=_params(("parallel",)),
        name="in_proj",
    )(*ins, *prev)


def _pool_kernel(a_ref, pre_ref, w_ref, s_ref, o_ref, u_scr, *, pos0, chunk):
    t, width = a_ref.shape
    pad = POOL_STATE + 1
    levels = len(POOL_WINDOWS)
    assert POOL_WINDOWS == tuple(2 ** (k + 1) for k in range(levels))
    u_scr[0, 0:1, :] = jnp.zeros((1, width), F32)
    u_scr[0, 1:pad, :] = pre_ref[...]
    u_scr[0, pad:pad + t, :] = a_ref[...]
    for k in range(1, levels):
        w, lo = 2 ** (k - 1), 2 ** k - 1
        u_scr[k, lo:pad, :] = u_scr[k - 1, lo:pad, :] + u_scr[k - 1, lo - w:pad - w, :]
    group = width // levels
    lane_g = lax.broadcasted_iota(jnp.int32, (1, width), 1) // group
    for c0 in range(0, t, chunk):
        r0, r1 = pad + c0, pad + c0 + chunk
        pos = pos0 + c0 + lax.broadcasted_iota(jnp.int32, (chunk, 1), 0)
        a = u_scr[0, r0:r1, :]
        pooled = jnp.zeros_like(a)
        for k in range(1, levels + 1):
            w = 2 ** (k - 1)
            s = u_scr[k - 1, r0:r1, :] + u_scr[k - 1, r0 - w:r1 - w, :]
            if k < levels:
                u_scr[k, r0:r1, :] = s
            cnt = jnp.minimum(2 * w, pos + 1).astype(F32)
            pooled = jnp.where(lane_g == k - 1, s / cnt, pooled)
        pooled = pooled - a
        o_ref[c0:c0 + chunk, :] = (_dot(pooled.astype(BF16), w_ref[...]) * s_ref[...]).astype(o_ref.dtype)


def _pool(a_in, prefix, w_bd, scale, b, t, pos0):
    width = a_in.shape[1]
    chunk = min(t, 256)
    return pl.pallas_call(
        functools.partial(_pool_kernel, pos0=pos0, chunk=chunk),
        grid=(b,),
        in_specs=[pl.BlockSpec((t, width), lambda i: (i, 0)),
                  pl.BlockSpec((None, POOL_STATE, width), lambda i: (i, 0, 0)),
                  _const_spec(w_bd.shape), _const_spec(scale.shape)],
        out_specs=pl.BlockSpec((t, width), lambda i: (i, 0)),
        out_shape=jax.ShapeDtypeStruct(a_in.shape, BF16),
        scratch_shapes=[pltpu.VMEM((len(POOL_WINDOWS), t + POOL_STATE + 1, width), F32)],
        compiler_params=_params(("parallel",)),
        name="pool",
    )(a_in, prefix, w_bd, scale)


def _lambda(lq1, lk1, lq2, lk2, lam_init):
    s1 = jnp.sum(lq1[...] * lk1[...], axis=-1, keepdims=True)
    s2 = jnp.sum(lq2[...] * lk2[...], axis=-1, keepdims=True)
    return jnp.exp(s1) - jnp.exp(s2) + lam_init


def _sub_norm(o, gain, lam_init):
    ms = jnp.mean(o * o, axis=-1, keepdims=True)
    return (o * lax.rsqrt(ms + EPS)) * gain * (1.0 - lam_init)


def _softplus(z):
    return jnp.maximum(z, 0.0) + jnp.log(1.0 + jnp.exp(-jnp.abs(z)))


def _split_bf16(x):
    hi = x.astype(BF16)
    lo = (x - hi.astype(F32)).astype(BF16)
    return hi, lo


def _pad_rows(x, rows):
    return jnp.concatenate([x, jnp.zeros((rows - x.shape[0], x.shape[1]), x.dtype)], axis=0)


def _lane_chunks(x):
    return [x[:, i * LANES:(i + 1) * LANES] for i in range(x.shape[1] // LANES)]


def _diff_attn_kernel(q_ref, k_ref, v_ref, lq1, lk1, lq2, lk2, sg_ref, o_ref, s_scr, *, tq, tk, heads, lam_init):
    hp = pl.program_id(1)
    qi = pl.program_id(2)
    hw = 2 * HEAD_DIM
    hpb = q_ref.shape[1] // hw
    lane = lax.broadcasted_iota(jnp.int32, (1, hw), 1)
    qs = []
    for j in range(hpb):
        q = q_ref[:, j * hw:(j + 1) * hw]
        zero = jnp.zeros_like(q)
        qs += [jnp.where(lane < HEAD_DIM, q, zero), jnp.where(lane >= HEAD_DIM, q, zero)]
    streams = range(2 * hpb)
    n_blk = (qi * tq + tq + tk - 1) // tk
    last = n_blk - 1

    def rows(ref, kb, j):
        start = pl.multiple_of(kb * (tk * heads), tk * heads) + hp * hpb + j
        return ref[pl.ds(start, tk, stride=heads), :].astype(BF16)

    def scores(kb, mx, masked):
        ks = [rows(k_ref, kb, j) for j in range(hpb)]
        col0 = pl.multiple_of(kb * tk, tk)
        s = [_dot_nt(qs[i], ks[i // 2]) for i in streams]
        if masked:
            row = qi * tq + lax.broadcasted_iota(jnp.int32, (tq, tk), 0)
            col = col0 + lax.broadcasted_iota(jnp.int32, (tq, tk), 1)
            s = [jnp.where(col <= row, s[i], NEG) for i in streams]
        for i in streams:
            s_scr[i, :, pl.ds(col0, tk)] = s[i]
        return tuple(functools.reduce(jnp.maximum, _lane_chunks(s[i]), mx[i]) for i in streams)

    neg = jnp.full((tq, LANES), NEG, F32)
    mx = lax.fori_loop(0, last, lambda kb, m: scores(kb, m, False), (neg,) * len(streams))
    mx = scores(last, mx, True)
    m_b = [jnp.broadcast_to(jnp.max(m, axis=-1, keepdims=True), (tq, LANES)) for m in mx]

    def accumulate(kb, carry):
        vs = [rows(v_ref, kb, j) for j in range(hpb)]
        col0 = pl.multiple_of(kb * tk, tk)
        p = [[jnp.exp2(sc - m_b[i]) for sc in _lane_chunks(s_scr[i, :, pl.ds(col0, tk)])] for i in streams]
        den = [functools.reduce(jnp.add, p[i], carry[i][0]) for i in streams]
        acc = [carry[i][1] + _dot(jnp.concatenate(p[i], axis=1).astype(BF16), vs[i // 2]) for i in streams]
        return tuple((den[i], acc[i]) for i in streams)

    zero_c = (jnp.zeros((tq, LANES), F32), jnp.zeros((tq, hw), F32))
    res = lax.fori_loop(0, n_blk, accumulate, (zero_c,) * len(streams))
    lam = _lambda(lq1, lk1, lq2, lk2, lam_init)
    norm = [acc / jnp.sum(den, axis=-1, keepdims=True) for den, acc in res]
    for j in range(hpb):
        o = norm[2 * j] - lam * norm[2 * j + 1]
        o_ref[:, j * hw:(j + 1) * hw] = _sub_norm(o, sg_ref[...], lam_init).astype(o_ref.dtype)


def _diff_attn_prompt(q, k, v, lam_vecs, sg, b, t, lam_init, tq, tk, hpb, layer):
    n, w = q.shape
    hw = 2 * HEAD_DIM
    heads = w // hw
    nq = t // tq
    vec = _const_spec((1, HEAD_DIM))
    kv = pl.BlockSpec((t * heads, hw), lambda bi, h, i: (layer * b + bi, 0))
    return pl.pallas_call(
        functools.partial(_diff_attn_kernel, tq=tq, tk=tk, heads=heads, lam_init=lam_init),
        grid=(b, heads // hpb, nq),
        in_specs=[pl.BlockSpec((tq, hpb * hw), lambda bi, h, i: (bi * nq + i, h)), kv, kv,
                  vec, vec, vec, vec, _const_spec((1, hw))],
        out_specs=pl.BlockSpec((tq, hpb * hw), lambda bi, h, i: (bi * nq + i, h)),
        out_shape=jax.ShapeDtypeStruct((n, w), BF16),
        scratch_shapes=[pltpu.VMEM((2 * hpb, tq, t), F32)],
        compiler_params=_params(("parallel", "arbitrary", "arbitrary")),
        name="diff_attn_prompt",
    )(q, k, v, *lam_vecs, sg)


def _stick_kernel(q_ref, k_ref, v_ref, o_ref, *, tq):
    qi = pl.program_id(1)
    q = q_ref[...]
    w = q.shape[1]
    lane_h = lax.broadcasted_iota(jnp.int32, (1, w), 1) // HEAD_DIM
    row = lax.broadcasted_iota(jnp.int32, (tq, tq), 0)
    col = lax.broadcasted_iota(jnp.int32, (tq, tq), 1)
    tri = jnp.where(row > col, 1.0, 0.0).astype(BF16)
    heads = w // HEAD_DIM
    qh = [jnp.where(lane_h == hh, q, jnp.zeros_like(q)) for hh in range(heads)]

    def step(kb, carry, diag):
        start = pl.multiple_of(kb * tq, tq)
        k = k_ref[:, pl.ds(start, tq)].astype(BF16)
        v = v_ref[:, pl.ds(start, tq)].astype(BF16)
        hs = range(heads)
        z = [_dot(qh[hh], k) for hh in hs]
        sp = [_softplus(z[hh]) for hh in hs]
        lk = [-sp[hh] for hh in hs]
        if diag:
            lk = [jnp.where(col < row, lk[hh], 0.0) for hh in hs]
        later = [_dot(lk[hh].astype(BF16), tri) + carry[hh][1] for hh in hs]
        a = [jnp.exp(z[hh] - sp[hh] + later[hh]) for hh in hs]
        if diag:
            a = [jnp.where(col < row, a[hh], 0.0) for hh in hs]
        acc = [carry[hh][0] + _dot_nt(a[hh].astype(BF16), v[hh * HEAD_DIM:(hh + 1) * HEAD_DIM, :]) for hh in hs]
        run = [carry[hh][1] + jnp.sum(lk[hh], axis=-1, keepdims=True) for hh in hs]
        return tuple((acc[hh], run[hh]) for hh in hs)

    init = tuple((jnp.zeros((tq, HEAD_DIM), F32), jnp.zeros((tq, 1), F32)) for _ in range(heads))
    carry = step(qi, init, True)
    carry = lax.fori_loop(0, qi, lambda j, c: step(qi - 1 - j, c, False), carry)
    o_ref[...] = jnp.concatenate([c[0] for c in carry], axis=1).astype(o_ref.dtype)


def _stick_prompt(q, k_t, v_t, b, t, tq, layer):
    n, w = q.shape
    nq = t // tq
    kv = pl.BlockSpec((w, t), lambda bi, i: (layer * b + bi, 0))
    return pl.pallas_call(
        functools.partial(_stick_kernel, tq=tq),
        grid=(b, nq),
        in_specs=[pl.BlockSpec((tq, w), lambda bi, i: (bi * nq + i, 0)), kv, kv],
        out_specs=pl.BlockSpec((tq, w), lambda bi, i: (bi * nq + i, 0)),
        out_shape=jax.ShapeDtypeStruct((n, w), BF16),
        compiler_params=_params(("parallel", "arbitrary")),
        name="stick_prompt",
    )(q, k_t, v_t)


def _head_rows(x, heads):
    return jnp.concatenate([x[:, h * LANES:(h + 1) * LANES] for h in range(heads)], axis=0)


def _diff_decode_kernel(pt_ref, q_ref, kn_ref, vn_ref, *rest, n_pages, lam_init):
    del pt_ref
    kp = rest[:n_pages]
    vp = rest[n_pages:2 * n_pages]
    lq1, lk1, lq2, lk2, sg_ref, o_ref, s_scr = rest[2 * n_pages:]
    t, w = q_ref.shape
    hw = 2 * HEAD_DIM
    heads = w // hw
    prow = kp[0].shape[0]
    nq = 2 * heads * t
    q = q_ref[...].astype(F32)
    lane = lax.broadcasted_iota(jnp.int32, (1, hw), 1)
    pieces = []
    for h in range(heads):
        qh = q[:, h * hw:(h + 1) * hw]
        pieces += [jnp.where(lane < HEAD_DIM, qh, 0.0), jnp.where(lane >= HEAD_DIM, qh, 0.0)]
    qrows = jnp.concatenate(pieces, axis=0).astype(BF16)
    row_h = lax.broadcasted_iota(jnp.int32, (nq, prow), 0) // (2 * t)
    col_h = lax.broadcasted_iota(jnp.int32, (nq, prow), 1) % heads
    page_mask = row_h == col_h
    for j in range(n_pages):
        s = _dot_nt(qrows, kp[j][...].astype(BF16))
        s_scr[:, j * prow:(j + 1) * prow] = jnp.where(page_mask, s, NEG)
    k_new = _pad_rows(_head_rows(kn_ref[...], heads), LANES).astype(BF16)
    v_new = _pad_rows(_head_rows(vn_ref[...], heads), LANES).astype(BF16)
    r = lax.broadcasted_iota(jnp.int32, (nq, LANES), 0)
    c = lax.broadcasted_iota(jnp.int32, (nq, LANES), 1)
    new_mask = (c < heads * t) & (r // (2 * t) == c // t) & (c % t <= r % t)
    s_scr[:, n_pages * prow:] = jnp.where(new_mask, _dot_nt(qrows, k_new), NEG)
    m = jnp.max(s_scr[...], axis=-1, keepdims=True)
    acc = jnp.zeros((nq, hw), F32)
    den = jnp.zeros((nq, 1), F32)
    for j in range(n_pages + 1):
        width = prow if j < n_pages else LANES
        p = jnp.exp2(s_scr[:, j * prow:j * prow + width] - m)
        den = den + jnp.sum(p, axis=-1, keepdims=True)
        v = vp[j][...].astype(BF16) if j < n_pages else v_new
        acc = acc + _dot(p.astype(BF16), v)
    o = acc / den
    lam = _lambda(lq1, lk1, lq2, lk2, lam_init)
    for h in range(heads):
        r0 = 2 * h * t
        oh = o[r0:r0 + t] - lam * o[r0 + t:r0 + 2 * t]
        o_ref[:, h * hw:(h + 1) * hw] = _sub_norm(oh, sg_ref[...], lam_init).astype(o_ref.dtype)


def _page_specs(cache, layer, n_pages):
    blk = (None, None) + cache.shape[2:]
    return [pl.BlockSpec(blk, lambda b, pt, j=j: (layer, pt[b, j], 0, 0)) for j in range(n_pages)]


def _diff_decode(q, k_new, v_new, cache_k, cache_v, page_table, layer, lam_vecs, sg, t, lam_init):
    n, w = q.shape
    b, n_pages = page_table.shape
    prow = cache_k.shape[2]
    heads = w // (2 * HEAD_DIM)
    tok = pl.BlockSpec((t, w), lambda bi, pt: (bi, 0))
    vec = pl.BlockSpec((1, HEAD_DIM), lambda bi, pt: (0, 0))
    grid_spec = pltpu.PrefetchScalarGridSpec(
        num_scalar_prefetch=1,
        grid=(b,),
        in_specs=[tok, tok, tok] + _page_specs(cache_k, layer, n_pages) + _page_specs(cache_v, layer, n_pages)
        + [vec, vec, vec, vec, pl.BlockSpec((1, 2 * HEAD_DIM), lambda bi, pt: (0, 0))],
        out_specs=tok,
        scratch_shapes=[pltpu.VMEM((2 * heads * t, n_pages * prow + LANES), F32)],
    )
    return pl.pallas_call(
        functools.partial(_diff_decode_kernel, n_pages=n_pages, lam_init=lam_init),
        grid_spec=grid_spec,
        out_shape=jax.ShapeDtypeStruct((n, w), BF16),
        compiler_params=_params(("parallel",)),
        name="diff_decode",
    )(page_table, q, k_new, v_new, *([cache_k] * n_pages), *([cache_v] * n_pages), *lam_vecs, sg)


def _stick_decode_kernel(pt_ref, q_ref, kn_ref, vn_ref, *rest, n_pages):
    del pt_ref
    kp = rest[:n_pages]
    vp = rest[n_pages:2 * n_pages]
    o_ref = rest[2 * n_pages]
    t, w = q_ref.shape
    heads = w // HEAD_DIM
    page = kp[0].shape[1]
    nq = heads * t
    q = q_ref[...].astype(F32)
    lane_h = lax.broadcasted_iota(jnp.int32, (1, w), 1) // HEAD_DIM
    qrows = jnp.concatenate([jnp.where(lane_h == h, q, 0.0) for h in range(heads)], axis=0).astype(BF16)
    r = lax.broadcasted_iota(jnp.int32, (page, page), 0)
    c = lax.broadcasted_iota(jnp.int32, (page, page), 1)
    tri = jnp.where(r > c, 1.0, 0.0).astype(BF16)
    qr = lax.broadcasted_iota(jnp.int32, (nq, page), 0) % t
    kc = lax.broadcasted_iota(jnp.int32, (nq, page), 1)
    new_mask = kc < qr
    pages = range(n_pages + 1)
    z = [_dot(qrows, kp[j][...].astype(BF16)) for j in range(n_pages)]
    z.append(_dot_nt(qrows, _pad_rows(kn_ref[...], page).astype(BF16)))
    sp = [_softplus(z[j]) for j in pages]
    lk = [-sp[j] for j in pages]
    lk[n_pages] = jnp.where(new_mask, lk[n_pages], 0.0)
    parts = [_split_bf16(lk[j]) for j in pages]
    within = [_dot(parts[j][0], tri) + _dot(parts[j][1], tri) for j in pages]
    totals = [jnp.sum(lk[j], axis=-1, keepdims=True) for j in pages]
    run = jnp.zeros((nq, 1), F32)
    a = [None] * (n_pages + 1)
    for j in range(n_pages, -1, -1):
        a[j] = jnp.exp(z[j] - sp[j] + within[j] + run)
        run = run + totals[j]
    a[n_pages] = jnp.where(new_mask, a[n_pages], 0.0)
    acc = _dot(a[n_pages].astype(BF16), _pad_rows(vn_ref[...], page).astype(BF16))
    for j in range(n_pages):
        acc = acc + _dot_nt(a[j].astype(BF16), vp[j][...].astype(BF16))
    o = jnp.zeros((t, w), F32)
    for h in range(heads):
        o = jnp.where(lane_h == h, acc[h * t:(h + 1) * t], o)
    o_ref[...] = o.astype(o_ref.dtype)


def _stick_decode(q, k_new, v_new, cache_k, cache_v, page_table, layer, t):
    n, w = q.shape
    b, n_pages = page_table.shape
    tok = pl.BlockSpec((t, w), lambda bi, pt: (bi, 0))
    grid_spec = pltpu.PrefetchScalarGridSpec(
        num_scalar_prefetch=1,
        grid=(b,),
        in_specs=[tok, tok, tok] + _page_specs(cache_k, layer, n_pages) + _page_specs(cache_v, layer, n_pages),
        out_specs=tok,
    )
    return pl.pallas_call(
        functools.partial(_stick_decode_kernel, n_pages=n_pages),
        grid_spec=grid_spec,
        out_shape=jax.ShapeDtypeStruct((n, w), BF16),
        compiler_params=_params(("parallel",)),
        name="stick_decode",
    )(page_table, q, k_new, v_new, *([cache_k] * n_pages), *([cache_v] * n_pages))


def _merge_kernel(x_ref, oa_ref, ob_ref, oc_ref, gate_ref, wa_ref, wb_ref, wc_ref, wo_ref,
                  g2_ref, wr_hi_ref, wr_lo_ref, br_ref, x_out, h_out, comb_out):
    d = x_ref.shape[1]
    gate = gate_ref[...].astype(F32)
    merged = (gate[:, 0:d] * _dot(oa_ref[...], wa_ref[...])
              + gate[:, d:2 * d] * _dot(ob_ref[...], wb_ref[...])
              + gate[:, 2 * d:3 * d] * _dot(oc_ref[...], wc_ref[...]))
    x = x_ref[...] + _dot(merged.astype(BF16), wo_ref[...])
    x_out[...] = x
    ms = jnp.mean(x * x, axis=-1, keepdims=True)
    h = (x * lax.rsqrt(ms + EPS)) * g2_ref[...]
    h_out[...] = h.astype(h_out.dtype)
    h_hi, h_lo = _split_bf16(h)
    logits = (_dot(h_hi, wr_hi_ref[...]) + _dot(h_lo, wr_hi_ref[...]) + _dot(h_hi, wr_lo_ref[...])
              + br_ref[...])
    lane = lax.broadcasted_iota(jnp.int32, logits.shape, 1).astype(F32)
    gl = jnp.where(lane < N_GROUPS, logits, NEG)
    gmax = jnp.max(gl, axis=-1, keepdims=True)
    gidx = jnp.min(jnp.where(gl == gmax, lane, 1e9), axis=-1, keepdims=True)
    gprob = 1.0 / jnp.sum(jnp.where(lane < N_GROUPS, jnp.exp(logits - gmax), 0.0), axis=-1, keepdims=True)
    rel = lane - ROUTE_OFF - EXPERTS_PER_GROUP * gidx
    el = jnp.where((rel >= 0) & (rel < EXPERTS_PER_GROUP), logits, NEG)
    t1 = jnp.max(el, axis=-1, keepdims=True)
    i1 = jnp.min(jnp.where(el == t1, lane, 1e9), axis=-1, keepdims=True)
    el2 = jnp.where(lane == i1, NEG, el)
    t2 = jnp.max(el2, axis=-1, keepdims=True)
    i2 = jnp.min(jnp.where(el2 == t2, lane, 1e9), axis=-1, keepdims=True)
    e21 = jnp.exp(t2 - t1)
    w1 = gprob / (1.0 + e21)
    w2 = w1 * e21
    comb_out[...] = jnp.where(lane == 0, gidx, jnp.where(lane == i1, w1, jnp.where(lane == i2, w2, 0.0)))


def _merge(x, o_a, o_b, o_c, gates, wa, wb, wc, wo, g2, wr_hi, wr_lo, br, tm):
    n, d = x.shape
    row = lambda i: (i, 0)
    ins = [x, o_a, o_b, o_c, gates]
    consts = [wa, wb, wc, wo, g2, wr_hi, wr_lo, br]
    return pl.pallas_call(
        _merge_kernel,
        grid=(n // tm,),
        in_specs=[pl.BlockSpec((tm, a.shape[1]), row) for a in ins] + [_const_spec(c.shape) for c in consts],
        out_specs=[pl.BlockSpec((tm, d), row), pl.BlockSpec((tm, d), row), pl.BlockSpec((tm, LANES), row)],
        out_shape=[jax.ShapeDtypeStruct((n, d), F32), jax.ShapeDtypeStruct((n, d), BF16),
                   jax.ShapeDtypeStruct((n, LANES), F32)],
        compiler_params=_params(("parallel",)),
        name="merge",
    )(*ins, *consts)


def _dispatch_plan(comb, tm):
    n = comb.shape[0]
    nt = n // tm
    g = comb[:, 0].astype(jnp.int32).reshape(nt, tm)
    onehot = (g[:, :, None] == jnp.arange(N_GROUPS, dtype=jnp.int32)[None, None, :]).astype(jnp.int32)
    rank = jnp.cumsum(onehot, axis=1) - onehot
    counts = jnp.sum(onehot, axis=1)
    padded = (counts + SUBLANES - 1) // SUBLANES * SUBLANES
    gstart = jnp.cumsum(padded, axis=1) - padded
    dest = jnp.sum((gstart[:, None, :] + rank) * onehot, axis=2)
    return dest.reshape(n), gstart.reshape(-1), counts.reshape(-1)


def _moe_kernel(dest_ref, gstart_ref, count_ref, x_ref, h_ref, comb_ref, wg_ref, wu_ref, wd_ref, o_ref,
                hf, hs, cs, ys, *, tm, win):
    i = pl.program_id(0)
    g = pl.program_id(1)
    base = i * tm

    @pl.when(g == 0)
    def _sort():
        hf[...] = h_ref[...].astype(F32)
        hs[...] = jnp.zeros_like(hs)
        cs[...] = jnp.zeros_like(cs)
        ys[...] = jnp.zeros_like(ys)

        def body(r, carry):
            d = dest_ref[base + r]
            hs[pl.ds(d, 1), :] = hf[pl.ds(r, 1), :]
            cs[pl.ds(d, 1), :] = comb_ref[pl.ds(r, 1), :]
            return carry

        lax.fori_loop(0, tm, body, 0, unroll=8)

    gs = gstart_ref[i * N_GROUPS + g]
    n_win = (count_ref[i * N_GROUPS + g] + win - 1) // win
    lane = lax.broadcasted_iota(jnp.int32, (1, LANES), 1)

    def window(w, carry):
        r0 = pl.multiple_of(gs + w * win, SUBLANES)
        hw = hs[pl.ds(r0, win), :].astype(BF16)
        cw = cs[pl.ds(r0, win), :]
        experts = range(EXPERTS_PER_GROUP)
        gate = [_dot(hw, wg_ref[e]) for e in experts]
        up = [_dot(hw, wu_ref[e]) for e in experts]
        ce = [jnp.sum(jnp.where(lane == ROUTE_OFF + EXPERTS_PER_GROUP * g + e, cw, 0.0), axis=-1, keepdims=True)
              for e in experts]
        act = jnp.concatenate([(jax.nn.silu(gate[e]) * up[e] * ce[e]).astype(BF16) for e in experts], axis=1)
        ys[pl.ds(r0, win), :] += _dot(act, wd_ref[...])
        return carry

    lax.fori_loop(0, n_win, window, 0)

    @pl.when(g == pl.num_programs(1) - 1)
    def _unsort():
        def body(r, carry):
            d = dest_ref[base + r]
            hf[pl.ds(r, 1), :] = ys[pl.ds(d, 1), :]
            return carry

        lax.fori_loop(0, tm, body, 0, unroll=8)
        o_ref[...] = x_ref[...] + hf[...]


def _moe(x, h, comb, w_g, w_u, w_d, tm):
    n, d = x.shape
    win = tm // N_GROUPS + 48
    rows = tm + N_GROUPS * SUBLANES + win
    dest, gstart, counts = _dispatch_plan(comb, tm)
    per_group = pl.BlockSpec((EXPERTS_PER_GROUP,) + w_g.shape[1:], lambda i, g, *_: (g, 0, 0))
    row = lambda i, g, *_: (i, 0)
    grid_spec = pltpu.PrefetchScalarGridSpec(
        num_scalar_prefetch=3,
        grid=(n // tm, N_GROUPS),
        in_specs=[pl.BlockSpec((tm, d), row), pl.BlockSpec((tm, d), row), pl.BlockSpec((tm, LANES), row),
                  per_group, per_group,
                  pl.BlockSpec((None,) + w_d.shape[1:], lambda i, g, *_: (g, 0, 0))],
        out_specs=pl.BlockSpec((tm, d), row),
        scratch_shapes=[pltpu.VMEM((tm, d), F32), pltpu.VMEM((rows, d), F32),
                        pltpu.VMEM((rows, LANES), F32), pltpu.VMEM((rows, d), F32)],
    )
    return pl.pallas_call(
        functools.partial(_moe_kernel, tm=tm, win=win),
        grid_spec=grid_spec,
        out_shape=jax.ShapeDtypeStruct((n, d), F32),
        compiler_params=_params(("parallel", "arbitrary")),
        name="moe",
    )(dest, gstart, counts, x, h, comb, w_g, w_u, w_d)


def _rope_tables(positions):
    half = ROT_DIM // 2
    inv = ROPE_THETA ** (-jnp.arange(0, ROT_DIM, 2, dtype=F32) / ROT_DIM)
    ang = positions.astype(F32)[:, None] * inv[None, :]
    cos, sin = jnp.cos(ang), jnp.sin(ang)
    n = positions.shape[0]
    rest = HEAD_DIM - ROT_DIM
    comp_c = jnp.concatenate([cos, cos, jnp.ones((n, rest), F32)], axis=1)
    comp_a = jnp.concatenate([-sin, jnp.zeros((n, half + rest), F32)], axis=1)
    comp_b = jnp.concatenate([jnp.zeros((n, half), F32), sin, jnp.zeros((n, rest), F32)], axis=1)
    return tuple(jnp.tile(c, (1, LANES // HEAD_DIM)) for c in (comp_c, comp_a, comp_b))


def _block_diag(blocks):
    g, a, b = blocks.shape
    tiled = jnp.tile(blocks.reshape(g * a, b), (1, g))
    rg = lax.broadcasted_iota(jnp.int32, (g * a, g * b), 0) // a
    cg = lax.broadcasted_iota(jnp.int32, (g * a, g * b), 1) // b
    return jnp.where(rg == cg, tiled, jnp.zeros_like(tiled))


def kernel(x_prompt, x_sample, cache_kb, cache_vb, cache_kc, cache_vc, state_pool, page_table, norm1_g, w_in, pool_w, pool_scale, qn_g, kn_g, lam_q1, lam_k1, lam_q2, lam_k2, subln_g, w_proj_a, w_proj_b, w_proj_c, w_out, norm2_g, w_group, b_group, w_router, b_router, w_e_gate, w_e_up, w_e_down):
    bp, tp, d = x_prompt.shape
    bs, ts, _ = x_sample.shape
    depth = w_in.shape[0]
    n_pages, page = page_table.shape[1], cache_kb.shape[2]
    past_len = n_pages * page
    pool_width = pool_w.shape[1] * pool_w.shape[2]
    diff_heads, stick_heads = cache_kb.shape[3], cache_kc.shape[3]
    diff_w = diff_heads * 2 * HEAD_DIM
    stick_w = stick_heads * HEAD_DIM
    sizes = (pool_width, diff_w, diff_w, diff_w, stick_w, stick_w, stick_w, d, d, d)

    tm_p = min(512, tp)
    tm_s = min(512, bs * ts)
    tq = min(256, tp)
    tk = min(512, tp)
    tm_moe_p = min(1024, bp * tp)
    tm_moe_s = min(1024, bs * ts)
    tabs_p = _rope_tables(jnp.arange(tp))
    tabs_s = _rope_tables(past_len + (jnp.arange(tm_s) % ts))
    pos_block_p = lambda i: i % (tp // tm_p)
    pos_block_s = lambda i: 0

    gm = _block_diag(jnp.full((diff_w // HEAD_DIM, HEAD_DIM, HEAD_DIM), 1.0 / HEAD_DIM, F32)).astype(BF16)
    n_pool_pages = cache_kb.shape[1]
    ckb, cvb = (c.reshape(depth, n_pool_pages, page * diff_heads, 2 * HEAD_DIM) for c in (cache_kb, cache_vb))
    ckc, cvc = (c.transpose(0, 1, 3, 4, 2).reshape(depth, n_pool_pages, stick_w, page) for c in (cache_kc, cache_vc))
    zero_pool = jnp.zeros((bp, POOL_STATE, pool_width), F32)

    xp = x_prompt.reshape(bp * tp, d)
    xs = x_sample.reshape(bs * ts, d)
    pool_rows_p, rows_s = [], []
    cache_p = ()
    for l in range(depth):
        lam_init = 0.8 - 0.6 * math.exp(-0.3 * l)
        g1 = norm1_g[l].reshape(1, d)
        w_in_l = w_in[l].astype(BF16)
        qg = jnp.tile(qn_g[l].reshape(1, -1), (1, diff_heads))
        kg = jnp.tile(kn_g[l].reshape(1, -1), (1, diff_heads))
        w_bd = _block_diag(pool_w[l]).astype(BF16)
        p_scale = pool_scale[l].reshape(1, -1)
        lam_vecs = tuple(v[l].reshape(1, -1) for v in (lam_q1, lam_k1, lam_q2, lam_k2))
        sg = subln_g[l].reshape(1, -1)
        wa, wb, wc, wo = (w[l].astype(BF16) for w in (w_proj_a, w_proj_b, w_proj_c, w_out))
        g2 = norm2_g[l].reshape(1, d)
        wr = jnp.concatenate([w_group[l], w_router[l],
                              jnp.zeros((d, LANES - N_GROUPS - N_EXPERTS), F32)], axis=1)
        wr_hi = wr.astype(BF16)
        wr_lo = (wr - wr_hi.astype(F32)).astype(BF16)
        br = jnp.concatenate([b_group[l], b_router[l],
                              jnp.zeros((LANES - N_GROUPS - N_EXPERTS,), F32)]).reshape(1, LANES)
        w_g, w_u = w_e_gate[l].astype(BF16), w_e_up[l].astype(BF16)
        w_d = w_e_down[l].reshape(N_GROUPS, -1, d).astype(BF16)

        def tail(x, o_a, o_b, o_c, gates, tm, tm_moe):
            x_mid, h2, comb = _merge(x, o_a, o_b, o_c, gates, wa, wb, wc, wo, g2, wr_hi, wr_lo, br, tm)
            return _moe(x_mid, h2, comb, w_g, w_u, w_d, tm_moe)

        a_in, qb, kb, vb, qc, kc, vc, gates = _in_proj(
            xp, g1, w_in_l, qg, kg, gm, tabs_p, pos_block_p, sizes, tm_p, tp, True, l, depth, cache_p)
        cache_p = (kb, vb, kc, vc)
        o_a = _pool(a_in, zero_pool, w_bd, p_scale, bp, tp, 0)
        o_b = _diff_attn_prompt(qb, kb, vb, lam_vecs, sg, bp, tp, lam_init, tq, tk, 4, l)
        o_c = _stick_prompt(qc, kc, vc, bp, tp, tq, l)
        xp = tail(xp, o_a, o_b, o_c, gates, tm_p, tm_moe_p)
        pool_rows_p.append(a_in.reshape(bp, tp, -1)[:, tp - POOL_STATE:])

        a_in, qb, kb, vb, qc, kc, vc, gates = _in_proj(
            xs, g1, w_in_l, qg, kg, gm, tabs_s, pos_block_s, sizes, tm_s, ts, False)
        o_a = _pool(a_in, state_pool[l], w_bd, p_scale, bs, ts, past_len)
        o_b = _diff_decode(qb, kb, vb, ckb, cvb, page_table, l, lam_vecs, sg, ts, lam_init)
        o_c = _stick_decode(qc, kc, vc, ckc, cvc, page_table, l, ts)
        xs = tail(xs, o_a, o_b, o_c, gates, tm_s, tm_moe_s)
        new_pool = jnp.concatenate([state_pool[l], a_in.reshape(bs, ts, -1)], axis=1)[:, -POOL_STATE:]
        rows_s.append((kb, vb, kc, vc, new_pool))

    kb_p, vb_p, kc_p, vc_p = cache_p
    pool_p = jnp.stack(pool_rows_p)
    kb_s, vb_s, kc_s, vc_s, pool_s = [jnp.stack(z) for z in zip(*rows_s)]
    diff_shape = lambda b, t: (depth, b, t, diff_heads, 2 * HEAD_DIM)
    stick_t = lambda z: z.reshape(depth, bp, stick_heads, HEAD_DIM, tp).transpose(0, 1, 4, 2, 3)
    return (xp.reshape(bp, tp, d), xs.reshape(bs, ts, d),
            kb_p.reshape(diff_shape(bp, tp)), vb_p.reshape(diff_shape(bp, tp)), stick_t(kc_p), stick_t(vc_p), pool_p,
            kb_s.reshape(diff_shape(bs, ts)), vb_s.reshape(diff_shape(bs, ts)),
            kc_s.reshape(depth, bs, ts, stick_heads, HEAD_DIM), vc_s.reshape(depth, bs, ts, stick_heads, HEAD_DIM),
            pool_s)
```

```python
import functools
import math

import numpy as np
import jax
import jax.numpy as jnp
from jax import lax
from jax.experimental import pallas as pl
from jax.experimental.pallas import tpu as pltpu

F32 = jnp.float32
BF16 = jnp.bfloat16

HEAD_DIM = 64
ROT_DIM = HEAD_DIM // 4
ROPE_THETA = 500000.0
POOL_WINDOWS = (2, 4, 8, 16)
POOL_STATE = max(POOL_WINDOWS) - 1
N_GROUPS = 4
EXPERTS_PER_GROUP = 4
N_EXPERTS = N_GROUPS * EXPERTS_PER_GROUP
EPS = 1e-6
LANES = 128
SUBLANES = 8
ROUTE_OFF = N_GROUPS
VMEM_LIMIT = 56 * 1024 * 1024
NEG = -1e30
LOG2E = math.log2(math.e)


def _dot(a, b):
    return jnp.dot(a, b, preferred_element_type=F32)


def _dot_nt(a, b):
    return lax.dot_general(a, b, (((1,), (1,)), ((), ())), preferred_element_type=F32)


def _params(sem):
    return pltpu.CompilerParams(dimension_semantics=sem, vmem_limit_bytes=VMEM_LIMIT)


def _layer_spec(stacked, layer):
    tail = stacked.shape[1:]
    return pl.BlockSpec((None,) + tail, lambda *_: (layer,) + (0,) * len(tail), pipeline_mode=pl.Buffered(1))


def _const_spec(shape):
    return pl.BlockSpec(shape, lambda *_: (0,) * len(shape), pipeline_mode=pl.Buffered(1))


def _in_proj_kernel(x_ref, g_ref, w_ref, qg_ref, kg_ref, gm_ref, c_ref, sa_ref, sb_ref, *rest,
                    sizes, cache_layout, n_prev):
    a_out, q_out, k_out, v_out, qc_out, kc_out, vc_out, gate_out = rest[n_prev:]
    x = x_ref[...]
    tm = x.shape[0]
    ms = jnp.mean(x * x, axis=-1, keepdims=True)
    h = ((x * lax.rsqrt(ms + EPS)) * g_ref[...]).astype(BF16)
    offs = np.concatenate([[0], np.cumsum(sizes)]).tolist()

    def seg(i, j=None):
        j = i if j is None else j
        return _dot(h, w_ref[:, offs[i]:offs[j + 1]])

    qk_w = sizes[1]
    reps = qk_w // LANES
    cos_t = jnp.tile(c_ref[...], (1, reps))
    sin_a = jnp.tile(sa_ref[...], (1, reps))
    sin_b = jnp.tile(sb_ref[...], (1, reps))
    half = ROT_DIM // 2

    def qk_norm_rope(y, gain):
        gms = _dot((y * y).astype(BF16), gm_ref[...])
        yn = (y * lax.rsqrt(gms + EPS)) * gain
        return (yn * cos_t + pltpu.roll(yn, qk_w - half, 1) * sin_a
                + pltpu.roll(yn, half, 1) * sin_b)

    def put_rows(ref, val):
        if cache_layout:
            for hd in range(reps):
                ref[pl.ds(hd, tm, stride=reps), :] = val[:, hd * LANES:(hd + 1) * LANES]
        else:
            ref[...] = val

    def put_cols(ref, val):
        ref[...] = val.T if cache_layout else val

    scale = HEAD_DIM ** -0.5
    a_out[...] = seg(0)
    q_out[...] = (qk_norm_rope(seg(1), qg_ref[...]) * (scale * LOG2E)).astype(q_out.dtype)
    put_rows(k_out, qk_norm_rope(seg(2), kg_ref[...]))
    put_rows(v_out, seg(3))
    qc_out[...] = (seg(4) * scale).astype(qc_out.dtype)
    put_cols(kc_out, seg(5))
    put_cols(vc_out, seg(6))
    gate_out[...] = jax.nn.sigmoid(seg(7, 9)).astype(gate_out.dtype)


def _in_proj(x, g1, w_in, qg, kg, gm, rope_tabs, pos_block, sizes, tm, seq_len, cache_layout,
             layer=0, depth=1, prev=()):
    n, d = x.shape
    cos_t, sin_a, sin_b = rope_tabs
    row = lambda i: (i, 0)
    tab = pl.BlockSpec((tm, LANES), lambda i: (pos_block(i), 0))
    diff_w, stick_w = sizes[1], sizes[4]
    heads = diff_w // LANES
    tiles = n // tm

    def plain(w, dt):
        return pl.BlockSpec((tm, w), row), jax.ShapeDtypeStruct((n, w), dt)

    if cache_layout:
        per_seq = seq_len // tm
        n_seq = n // seq_len
        rows = (pl.BlockSpec((tm * heads, LANES), lambda i: (layer * tiles + i, 0)),
                jax.ShapeDtypeStruct((depth * n * heads, LANES), F32))
        cols = (pl.BlockSpec((stick_w, tm), lambda i: (layer * n_seq + i // per_seq, i % per_seq)),
                jax.ShapeDtypeStruct((depth * n_seq * stick_w, seq_len), F32))
    else:
        rows, cols = plain(diff_w, F32), plain(stick_w, F32)
    outs = [plain(sizes[0], F32), plain(diff_w, BF16), rows, rows,
            plain(stick_w, BF16), cols, cols, plain(sum(sizes[7:]), BF16)]
    ins = [x, g1, w_in, qg, kg, gm, cos_t, sin_a, sin_b]
    cache_outs = (2, 3, 5, 6)
    return pl.pallas_call(
        functools.partial(_in_proj_kernel, sizes=tuple(sizes), cache_layout=cache_layout, n_prev=len(prev)),
        grid=(tiles,),
        in_specs=[pl.BlockSpec((tm, d), row), _const_spec((1, d)), _layer_spec(w_in, layer),
                  _const_spec(qg.shape), _const_spec(kg.shape), _const_spec(gm.shape), tab, tab, tab]
        + [pl.BlockSpec(memory_space=pl.ANY)] * len(prev),
        out_specs=[o[0] for o in outs],
        out_shape=[o[1] for o in outs],
        input_output_aliases={len(ins) + j: cache_outs[j] for j in range(len(prev))},
        compiler_params=_params(("parallel",)),
        name="in_proj",
    )(*ins, *prev)


def _pool_kernel(a_ref, pre_ref, w_ref, s_ref, o_ref, u_scr, *, pos0, t, chunk):
    nb, width = pre_ref.shape[0], a_ref.shape[1]
    pad = POOL_STATE + 1
    levels = len(POOL_WINDOWS)
    assert POOL_WINDOWS == tuple(2 ** (k + 1) for k in range(levels))
    group = width // levels
    lane_g = lax.broadcasted_iota(jnp.int32, (1, width), 1) // group
    for b in range(nb):
        u_scr[0, 0:1, :] = jnp.zeros((1, width), F32)
        u_scr[0, 1:pad, :] = pre_ref[b]
        u_scr[0, pad:pad + t, :] = a_ref[b * t:(b + 1) * t, :]
        for k in range(1, levels):
            w, lo = 2 ** (k - 1), 2 ** k - 1
            u_scr[k, lo:pad, :] = u_scr[k - 1, lo:pad, :] + u_scr[k - 1, lo - w:pad - w, :]
        for c0 in range(0, t, chunk):
            r0, r1 = pad + c0, pad + c0 + chunk
            pos = pos0 + c0 + lax.broadcasted_iota(jnp.int32, (chunk, 1), 0)
            a = u_scr[0, r0:r1, :]
            pooled = jnp.zeros_like(a)
            for k in range(1, levels + 1):
                w = 2 ** (k - 1)
                s = u_scr[k - 1, r0:r1, :] + u_scr[k - 1, r0 - w:r1 - w, :]
                if k < levels:
                    u_scr[k, r0:r1, :] = s
                cnt = jnp.minimum(2 * w, pos + 1).astype(F32)
                pooled = jnp.where(lane_g == k - 1, s / cnt, pooled)
            pooled = pooled - a
            o_ref[b * t + c0:b * t + c0 + chunk, :] = (
                _dot(pooled.astype(BF16), w_ref[...]) * s_ref[...]).astype(o_ref.dtype)


def _pool(a_in, prefix, w_bd, scale, b, t, pos0):
    width = a_in.shape[1]
    chunk = min(t, 256)
    nb = max(1, min(b, LANES // t))
    return pl.pallas_call(
        functools.partial(_pool_kernel, pos0=pos0, t=t, chunk=chunk),
        grid=(b // nb,),
        in_specs=[pl.BlockSpec((nb * t, width), lambda i: (i, 0)),
                  pl.BlockSpec((nb, POOL_STATE, width), lambda i: (i, 0, 0)),
                  _const_spec(w_bd.shape), _const_spec(scale.shape)],
        out_specs=pl.BlockSpec((nb * t, width), lambda i: (i, 0)),
        out_shape=jax.ShapeDtypeStruct(a_in.shape, BF16),
        scratch_shapes=[pltpu.VMEM((len(POOL_WINDOWS), t + POOL_STATE + 1, width), F32)],
        compiler_params=_params(("parallel",)),
        name="pool",
    )(a_in, prefix, w_bd, scale)


def _lambda(lq1, lk1, lq2, lk2, lam_init):
    s1 = jnp.sum(lq1[...] * lk1[...], axis=-1, keepdims=True)
    s2 = jnp.sum(lq2[...] * lk2[...], axis=-1, keepdims=True)
    return jnp.exp(s1) - jnp.exp(s2) + lam_init


def _sub_norm(o, gain, lam_init):
    ms = jnp.mean(o * o, axis=-1, keepdims=True)
    return (o * lax.rsqrt(ms + EPS)) * gain * (1.0 - lam_init)


def _softplus(z):
    return jnp.maximum(z, 0.0) + jnp.log(1.0 + jnp.exp(-jnp.abs(z)))


def _split_bf16(x):
    hi = x.astype(BF16)
    lo = (x - hi.astype(F32)).astype(BF16)
    return hi, lo


def _pad_rows(x, rows):
    return jnp.concatenate([x, jnp.zeros((rows - x.shape[0], x.shape[1]), x.dtype)], axis=0)


def _lane_chunks(x):
    return [x[:, i * LANES:(i + 1) * LANES] for i in range(x.shape[1] // LANES)]


def _diff_attn_kernel(q_ref, k_ref, v_ref, lq1, lk1, lq2, lk2, sg_ref, o_ref, s_scr, *, tq, tk, heads, lam_init):
    hp = pl.program_id(1)
    qi = pl.program_id(2)
    hw = 2 * HEAD_DIM
    hpb = q_ref.shape[1] // hw
    lane = lax.broadcasted_iota(jnp.int32, (1, hw), 1)
    qs = []
    for j in range(hpb):
        q = q_ref[:, j * hw:(j + 1) * hw]
        zero = jnp.zeros_like(q)
        qs += [jnp.where(lane < HEAD_DIM, q, zero), jnp.where(lane >= HEAD_DIM, q, zero)]
    streams = range(2 * hpb)
    n_blk = (qi * tq + tq + tk - 1) // tk
    last = n_blk - 1

    def rows(ref, kb, j):
        start = pl.multiple_of(kb * (tk * heads), tk * heads) + hp * hpb + j
        return ref[pl.ds(start, tk, stride=heads), :].astype(BF16)

    def scores(kb, mx, masked):
        ks = [rows(k_ref, kb, j) for j in range(hpb)]
        col0 = pl.multiple_of(kb * tk, tk)
        s = [_dot_nt(qs[i], ks[i // 2]) for i in streams]
        if masked:
            row = qi * tq + lax.broadcasted_iota(jnp.int32, (tq, tk), 0)
            col = col0 + lax.broadcasted_iota(jnp.int32, (tq, tk), 1)
            s = [jnp.where(col <= row, s[i], NEG) for i in streams]
        for i in streams:
            s_scr[i, :, pl.ds(col0, tk)] = s[i]
        return tuple(functools.reduce(jnp.maximum, _lane_chunks(s[i]), mx[i]) for i in streams)

    neg = jnp.full((tq, LANES), NEG, F32)
    mx = lax.fori_loop(0, last, lambda kb, m: scores(kb, m, False), (neg,) * len(streams))
    mx = scores(last, mx, True)
    m_b = [jnp.broadcast_to(jnp.max(m, axis=-1, keepdims=True), (tq, LANES)) for m in mx]

    def accumulate(kb, carry):
        vs = [rows(v_ref, kb, j) for j in range(hpb)]
        col0 = pl.multiple_of(kb * tk, tk)
        p = [[jnp.exp2(sc - m_b[i]) for sc in _lane_chunks(s_scr[i, :, pl.ds(col0, tk)])] for i in streams]
        den = [functools.reduce(jnp.add, p[i], carry[i][0]) for i in streams]
        acc = [carry[i][1] + _dot(jnp.concatenate(p[i], axis=1).astype(BF16), vs[i // 2]) for i in streams]
        return tuple((den[i], acc[i]) for i in streams)

    zero_c = (jnp.zeros((tq, LANES), F32), jnp.zeros((tq, hw), F32))
    res = lax.fori_loop(0, n_blk, accumulate, (zero_c,) * len(streams))
    lam = _lambda(lq1, lk1, lq2, lk2, lam_init)
    norm = [acc / jnp.sum(den, axis=-1, keepdims=True) for den, acc in res]
    for j in range(hpb):
        o = norm[2 * j] - lam * norm[2 * j + 1]
        o_ref[:, j * hw:(j + 1) * hw] = _sub_norm(o, sg_ref[...], lam_init).astype(o_ref.dtype)


def _diff_attn_prompt(q, k, v, lam_vecs, sg, b, t, lam_init, tq, tk, hpb, layer):
    n, w = q.shape
    hw = 2 * HEAD_DIM
    heads = w // hw
    nq = t // tq
    vec = _const_spec((1, HEAD_DIM))
    kv = pl.BlockSpec((t * heads, hw), lambda bi, h, i: (layer * b + bi, 0))
    return pl.pallas_call(
        functools.partial(_diff_attn_kernel, tq=tq, tk=tk, heads=heads, lam_init=lam_init),
        grid=(b, heads // hpb, nq),
        in_specs=[pl.BlockSpec((tq, hpb * hw), lambda bi, h, i: (bi * nq + i, h)), kv, kv,
                  vec, vec, vec, vec, _const_spec((1, hw))],
        out_specs=pl.BlockSpec((tq, hpb * hw), lambda bi, h, i: (bi * nq + i, h)),
        out_shape=jax.ShapeDtypeStruct((n, w), BF16),
        scratch_shapes=[pltpu.VMEM((2 * hpb, tq, t), F32)],
        compiler_params=_params(("parallel", "arbitrary", "arbitrary")),
        name="diff_attn_prompt",
    )(q, k, v, *lam_vecs, sg)


def _stick_kernel(q_ref, k_ref, v_ref, o_ref, *, tq):
    qi = pl.program_id(1)
    q = q_ref[...]
    w = q.shape[1]
    lane_h = lax.broadcasted_iota(jnp.int32, (1, w), 1) // HEAD_DIM
    row = lax.broadcasted_iota(jnp.int32, (tq, tq), 0)
    col = lax.broadcasted_iota(jnp.int32, (tq, tq), 1)
    tri = jnp.where(row > col, 1.0, 0.0).astype(BF16)
    heads = w // HEAD_DIM
    qh = [jnp.where(lane_h == hh, q, jnp.zeros_like(q)) for hh in range(heads)]

    def step(kb, carry, diag):
        start = pl.multiple_of(kb * tq, tq)
        k = k_ref[:, pl.ds(start, tq)].astype(BF16)
        v = v_ref[:, pl.ds(start, tq)].astype(BF16)
        hs = range(heads)
        z = [_dot(qh[hh], k) for hh in hs]
        sp = [_softplus(z[hh]) for hh in hs]
        lk = [-sp[hh] for hh in hs]
        if diag:
            lk = [jnp.where(col < row, lk[hh], 0.0) for hh in hs]
        later = [_dot(lk[hh].astype(BF16), tri) + carry[hh][1] for hh in hs]
        a = [jnp.exp(z[hh] - sp[hh] + later[hh]) for hh in hs]
        if diag:
            a = [jnp.where(col < row, a[hh], 0.0) for hh in hs]
        acc = [carry[hh][0] + _dot_nt(a[hh].astype(BF16), v[hh * HEAD_DIM:(hh + 1) * HEAD_DIM, :]) for hh in hs]
        run = [carry[hh][1] + jnp.sum(lk[hh], axis=-1, keepdims=True) for hh in hs]
        return tuple((acc[hh], run[hh]) for hh in hs)

    init = tuple((jnp.zeros((tq, HEAD_DIM), F32), jnp.zeros((tq, 1), F32)) for _ in range(heads))
    carry = step(qi, init, True)
    carry = lax.fori_loop(0, qi, lambda j, c: step(qi - 1 - j, c, False), carry)
    o_ref[...] = jnp.concatenate([c[0] for c in carry], axis=1).astype(o_ref.dtype)


def _stick_prompt(q, k_t, v_t, b, t, tq, layer):
    n, w = q.shape
    nq = t // tq
    kv = pl.BlockSpec((w, t), lambda bi, i: (layer * b + bi, 0))
    return pl.pallas_call(
        functools.partial(_stick_kernel, tq=tq),
        grid=(b, nq),
        in_specs=[pl.BlockSpec((tq, w), lambda bi, i: (bi * nq + i, 0)), kv, kv],
        out_specs=pl.BlockSpec((tq, w), lambda bi, i: (bi * nq + i, 0)),
        out_shape=jax.ShapeDtypeStruct((n, w), BF16),
        compiler_params=_params(("parallel", "arbitrary")),
        name="stick_prompt",
    )(q, k_t, v_t)


def _head_rows(x, heads):
    return jnp.concatenate([x[:, h * LANES:(h + 1) * LANES] for h in range(heads)], axis=0)


def _diff_decode_kernel(pt_ref, q_ref, kn_ref, vn_ref, *rest, n_pages, lam_init):
    del pt_ref
    kp = rest[:n_pages]
    vp = rest[n_pages:2 * n_pages]
    lq1, lk1, lq2, lk2, sg_ref, o_ref, s_scr = rest[2 * n_pages:]
    t, w = q_ref.shape
    hw = 2 * HEAD_DIM
    heads = w // hw
    prow = kp[0].shape[0]
    nq = 2 * heads * t
    q = q_ref[...].astype(F32)
    lane = lax.broadcasted_iota(jnp.int32, (1, hw), 1)
    pieces = []
    for h in range(heads):
        qh = q[:, h * hw:(h + 1) * hw]
        pieces += [jnp.where(lane < HEAD_DIM, qh, 0.0), jnp.where(lane >= HEAD_DIM, qh, 0.0)]
    qrows = jnp.concatenate(pieces, axis=0).astype(BF16)
    row_h = lax.broadcasted_iota(jnp.int32, (nq, prow), 0) // (2 * t)
    col_h = lax.broadcasted_iota(jnp.int32, (nq, prow), 1) % heads
    page_mask = row_h == col_h
    for j in range(n_pages):
        s = _dot_nt(qrows, kp[j][...].astype(BF16))
        s_scr[:, j * prow:(j + 1) * prow] = jnp.where(page_mask, s, NEG)
    k_new = _pad_rows(_head_rows(kn_ref[...], heads), LANES).astype(BF16)
    v_new = _pad_rows(_head_rows(vn_ref[...], heads), LANES).astype(BF16)
    r = lax.broadcasted_iota(jnp.int32, (nq, LANES), 0)
    c = lax.broadcasted_iota(jnp.int32, (nq, LANES), 1)
    new_mask = (c < heads * t) & (r // (2 * t) == c // t) & (c % t <= r % t)
    s_scr[:, n_pages * prow:] = jnp.where(new_mask, _dot_nt(qrows, k_new), NEG)
    m = jnp.max(s_scr[...], axis=-1, keepdims=True)
    acc = jnp.zeros((nq, hw), F32)
    den = jnp.zeros((nq, 1), F32)
    for j in range(n_pages + 1):
        width = prow if j < n_pages else LANES
        p = jnp.exp2(s_scr[:, j * prow:j * prow + width] - m)
        den = den + jnp.sum(p, axis=-1, keepdims=True)
        v = vp[j][...].astype(BF16) if j < n_pages else v_new
        acc = acc + _dot(p.astype(BF16), v)
    o = acc / den
    lam = _lambda(lq1, lk1, lq2, lk2, lam_init)
    for h in range(heads):
        r0 = 2 * h * t
        oh = o[r0:r0 + t] - lam * o[r0 + t:r0 + 2 * t]
        o_ref[:, h * hw:(h + 1) * hw] = _sub_norm(oh, sg_ref[...], lam_init).astype(o_ref.dtype)


def _page_specs(cache, layer, n_pages):
    blk = (None, None) + cache.shape[2:]
    return [pl.BlockSpec(blk, lambda b, pt, j=j: (layer, pt[b, j], 0, 0)) for j in range(n_pages)]


def _diff_decode(q, k_new, v_new, cache_k, cache_v, page_table, layer, lam_vecs, sg, t, lam_init):
    n, w = q.shape
    b, n_pages = page_table.shape
    prow = cache_k.shape[2]
    heads = w // (2 * HEAD_DIM)
    tok = pl.BlockSpec((t, w), lambda bi, pt: (bi, 0))
    vec = pl.BlockSpec((1, HEAD_DIM), lambda bi, pt: (0, 0))
    grid_spec = pltpu.PrefetchScalarGridSpec(
        num_scalar_prefetch=1,
        grid=(b,),
        in_specs=[tok, tok, tok] + _page_specs(cache_k, layer, n_pages) + _page_specs(cache_v, layer, n_pages)
        + [vec, vec, vec, vec, pl.BlockSpec((1, 2 * HEAD_DIM), lambda bi, pt: (0, 0))],
        out_specs=tok,
        scratch_shapes=[pltpu.VMEM((2 * heads * t, n_pages * prow + LANES), F32)],
    )
    return pl.pallas_call(
        functools.partial(_diff_decode_kernel, n_pages=n_pages, lam_init=lam_init),
        grid_spec=grid_spec,
        out_shape=jax.ShapeDtypeStruct((n, w), BF16),
        compiler_params=_params(("parallel",)),
        name="diff_decode",
    )(page_table, q, k_new, v_new, *([cache_k] * n_pages), *([cache_v] * n_pages), *lam_vecs, sg)


def _stick_decode_kernel(pt_ref, q_ref, kn_ref, vn_ref, *rest, n_pages):
    del pt_ref
    kp = rest[:n_pages]
    vp = rest[n_pages:2 * n_pages]
    o_ref = rest[2 * n_pages]
    t, w = q_ref.shape
    heads = w // HEAD_DIM
    page = kp[0].shape[1]
    nq = heads * t
    q = q_ref[...].astype(F32)
    lane_h = lax.broadcasted_iota(jnp.int32, (1, w), 1) // HEAD_DIM
    qrows = jnp.concatenate([jnp.where(lane_h == h, q, 0.0) for h in range(heads)], axis=0).astype(BF16)
    r = lax.broadcasted_iota(jnp.int32, (page, page), 0)
    c = lax.broadcasted_iota(jnp.int32, (page, page), 1)
    tri = jnp.where(r > c, 1.0, 0.0).astype(BF16)
    qr = lax.broadcasted_iota(jnp.int32, (nq, page), 0) % t
    kc = lax.broadcasted_iota(jnp.int32, (nq, page), 1)
    new_mask = kc < qr
    pages = range(n_pages + 1)
    z = [_dot(qrows, kp[j][...].astype(BF16)) for j in range(n_pages)]
    z.append(_dot_nt(qrows, _pad_rows(kn_ref[...], page).astype(BF16)))
    sp = [_softplus(z[j]) for j in pages]
    lk = [-sp[j] for j in pages]
    lk[n_pages] = jnp.where(new_mask, lk[n_pages], 0.0)
    within = [_dot(lk[j].astype(BF16), tri) for j in pages]
    totals = [jnp.sum(lk[j], axis=-1, keepdims=True) for j in pages]
    run = jnp.zeros((nq, 1), F32)
    a = [None] * (n_pages + 1)
    for j in range(n_pages, -1, -1):
        a[j] = jnp.exp(z[j] - sp[j] + within[j] + run)
        run = run + totals[j]
    a[n_pages] = jnp.where(new_mask, a[n_pages], 0.0)
    acc = _dot(a[n_pages].astype(BF16), _pad_rows(vn_ref[...], page).astype(BF16))
    for j in range(n_pages):
        acc = acc + _dot_nt(a[j].astype(BF16), vp[j][...].astype(BF16))
    o = jnp.zeros((t, w), F32)
    for h in range(heads):
        o = jnp.where(lane_h == h, acc[h * t:(h + 1) * t], o)
    o_ref[...] = o.astype(o_ref.dtype)


def _stick_decode(q, k_new, v_new, cache_k, cache_v, page_table, layer, t):
    n, w = q.shape
    b, n_pages = page_table.shape
    tok = pl.BlockSpec((t, w), lambda bi, pt: (bi, 0))
    grid_spec = pltpu.PrefetchScalarGridSpec(
        num_scalar_prefetch=1,
        grid=(b,),
        in_specs=[tok, tok, tok] + _page_specs(cache_k, layer, n_pages) + _page_specs(cache_v, layer, n_pages),
        out_specs=tok,
    )
    return pl.pallas_call(
        functools.partial(_stick_decode_kernel, n_pages=n_pages),
        grid_spec=grid_spec,
        out_shape=jax.ShapeDtypeStruct((n, w), BF16),
        compiler_params=_params(("parallel",)),
        name="stick_decode",
    )(page_table, q, k_new, v_new, *([cache_k] * n_pages), *([cache_v] * n_pages))


def _merge_kernel(x_ref, oa_ref, ob_ref, oc_ref, gate_ref, wa_ref, wb_ref, wc_ref, wo_ref,
                  g2_ref, wr_hi_ref, wr_lo_ref, br_ref, x_out, h_out, comb_out):
    d = x_ref.shape[1]
    gate = gate_ref[...].astype(F32)
    merged = (gate[:, 0:d] * _dot(oa_ref[...], wa_ref[...])
              + gate[:, d:2 * d] * _dot(ob_ref[...], wb_ref[...])
              + gate[:, 2 * d:3 * d] * _dot(oc_ref[...], wc_ref[...]))
    x = x_ref[...] + _dot(merged.astype(BF16), wo_ref[...])
    x_out[...] = x
    ms = jnp.mean(x * x, axis=-1, keepdims=True)
    h = (x * lax.rsqrt(ms + EPS)) * g2_ref[...]
    h_out[...] = h.astype(h_out.dtype)
    h_hi, h_lo = _split_bf16(h)
    logits = (_dot(h_hi, wr_hi_ref[...]) + _dot(h_lo, wr_hi_ref[...]) + _dot(h_hi, wr_lo_ref[...])
              + br_ref[...])
    lane = lax.broadcasted_iota(jnp.int32, logits.shape, 1).astype(F32)
    gl = jnp.where(lane < N_GROUPS, logits, NEG)
    gmax = jnp.max(gl, axis=-1, keepdims=True)
    gidx = jnp.min(jnp.where(gl == gmax, lane, 1e9), axis=-1, keepdims=True)
    gprob = 1.0 / jnp.sum(jnp.where(lane < N_GROUPS, jnp.exp(logits - gmax), 0.0), axis=-1, keepdims=True)
    rel = lane - ROUTE_OFF - EXPERTS_PER_GROUP * gidx
    el = jnp.where((rel >= 0) & (rel < EXPERTS_PER_GROUP), logits, NEG)
    t1 = jnp.max(el, axis=-1, keepdims=True)
    i1 = jnp.min(jnp.where(el == t1, lane, 1e9), axis=-1, keepdims=True)
    el2 = jnp.where(lane == i1, NEG, el)
    t2 = jnp.max(el2, axis=-1, keepdims=True)
    i2 = jnp.min(jnp.where(el2 == t2, lane, 1e9), axis=-1, keepdims=True)
    e21 = jnp.exp(t2 - t1)
    w1 = gprob / (1.0 + e21)
    w2 = w1 * e21
    comb_out[...] = jnp.where(lane == 0, gidx, jnp.where(lane == i1, w1, jnp.where(lane == i2, w2, 0.0)))


def _merge(x, o_a, o_b, o_c, gates, wa, wb, wc, wo, g2, wr_hi, wr_lo, br, tm, layer):
    n, d = x.shape
    row = lambda i: (i, 0)
    ins = [x, o_a, o_b, o_c, gates]
    stacks = [wa, wb, wc, wo]
    consts = [g2, wr_hi, wr_lo, br]
    return pl.pallas_call(
        _merge_kernel,
        grid=(n // tm,),
        in_specs=[pl.BlockSpec((tm, a.shape[1]), row) for a in ins] + [_layer_spec(s, layer) for s in stacks]
        + [_const_spec(c.shape) for c in consts],
        out_specs=[pl.BlockSpec((tm, d), row), pl.BlockSpec((tm, d), row), pl.BlockSpec((tm, LANES), row)],
        out_shape=[jax.ShapeDtypeStruct((n, d), F32), jax.ShapeDtypeStruct((n, d), BF16),
                   jax.ShapeDtypeStruct((n, LANES), F32)],
        compiler_params=_params(("parallel",)),
        name="merge",
    )(*ins, *stacks, *consts)


def _dispatch_plan(comb, tm):
    n = comb.shape[0]
    nt = n // tm
    g = comb[:, 0].astype(jnp.int32).reshape(nt, tm)
    onehot = (g[:, :, None] == jnp.arange(N_GROUPS, dtype=jnp.int32)[None, None, :]).astype(jnp.int32)
    rank = jnp.cumsum(onehot, axis=1) - onehot
    counts = jnp.sum(onehot, axis=1)
    padded = (counts + SUBLANES - 1) // SUBLANES * SUBLANES
    gstart = jnp.cumsum(padded, axis=1) - padded
    dest = jnp.sum((gstart[:, None, :] + rank) * onehot, axis=2)
    return dest.reshape(n), gstart.reshape(-1), counts.reshape(-1)


def _moe_kernel(dest_ref, gstart_ref, count_ref, x_ref, h_ref, comb_ref, wg_ref, wu_ref, wd_ref, o_ref,
                hf, hs, cs, ys, *, tm, win):
    i = pl.program_id(0)
    g = pl.program_id(1)
    base = i * tm

    @pl.when(g == 0)
    def _sort():
        hf[...] = h_ref[...].astype(F32)
        hs[...] = jnp.zeros_like(hs)
        cs[...] = jnp.zeros_like(cs)
        ys[...] = jnp.zeros_like(ys)

        def body(r, carry):
            d = dest_ref[base + r]
            hs[pl.ds(d, 1), :] = hf[pl.ds(r, 1), :]
            cs[pl.ds(d, 1), :] = comb_ref[pl.ds(r, 1), :]
            return carry

        lax.fori_loop(0, tm, body, 0, unroll=8)

    gs = gstart_ref[i * N_GROUPS + g]
    n_win = (count_ref[i * N_GROUPS + g] + win - 1) // win
    lane = lax.broadcasted_iota(jnp.int32, (1, LANES), 1)

    def window(w, carry):
        r0 = pl.multiple_of(gs + w * win, SUBLANES)
        hw = hs[pl.ds(r0, win), :].astype(BF16)
        cw = cs[pl.ds(r0, win), :]
        experts = range(EXPERTS_PER_GROUP)
        gate = [_dot(hw, wg_ref[e]) for e in experts]
        up = [_dot(hw, wu_ref[e]) for e in experts]
        ce = [jnp.sum(jnp.where(lane == ROUTE_OFF + EXPERTS_PER_GROUP * g + e, cw, 0.0), axis=-1, keepdims=True)
              for e in experts]
        act = jnp.concatenate([(jax.nn.silu(gate[e]) * up[e] * ce[e]).astype(BF16) for e in experts], axis=1)
        ys[pl.ds(r0, win), :] += _dot(act, wd_ref[...])
        return carry

    lax.fori_loop(0, n_win, window, 0)

    @pl.when(g == pl.num_programs(1) - 1)
    def _unsort():
        def body(r, carry):
            d = dest_ref[base + r]
            hf[pl.ds(r, 1), :] = ys[pl.ds(d, 1), :]
            return carry

        lax.fori_loop(0, tm, body, 0, unroll=8)
        o_ref[...] = x_ref[...] + hf[...]


def _moe(x, h, comb, w_g, w_u, w_d, tm, layer):
    n, d = x.shape
    win = tm // N_GROUPS + 48
    rows = tm + N_GROUPS * SUBLANES + win
    dest, gstart, counts = _dispatch_plan(comb, tm)
    per_group = pl.BlockSpec((None, EXPERTS_PER_GROUP) + w_g.shape[2:], lambda i, g, *_: (layer, g, 0, 0))
    row = lambda i, g, *_: (i, 0)
    grid_spec = pltpu.PrefetchScalarGridSpec(
        num_scalar_prefetch=3,
        grid=(n // tm, N_GROUPS),
        in_specs=[pl.BlockSpec((tm, d), row), pl.BlockSpec((tm, d), row), pl.BlockSpec((tm, LANES), row),
                  per_group, per_group,
                  pl.BlockSpec((None, None) + w_d.shape[2:], lambda i, g, *_: (layer, g, 0, 0))],
        out_specs=pl.BlockSpec((tm, d), row),
        scratch_shapes=[pltpu.VMEM((tm, d), F32), pltpu.VMEM((rows, d), F32),
                        pltpu.VMEM((rows, LANES), F32), pltpu.VMEM((rows, d), F32)],
    )
    return pl.pallas_call(
        functools.partial(_moe_kernel, tm=tm, win=win),
        grid_spec=grid_spec,
        out_shape=jax.ShapeDtypeStruct((n, d), F32),
        compiler_params=_params(("parallel", "arbitrary")),
        name="moe",
    )(dest, gstart, counts, x, h, comb, w_g, w_u, w_d)


def _rope_tables(positions):
    half = ROT_DIM // 2
    inv = ROPE_THETA ** (-jnp.arange(0, ROT_DIM, 2, dtype=F32) / ROT_DIM)
    ang = positions.astype(F32)[:, None] * inv[None, :]
    cos, sin = jnp.cos(ang), jnp.sin(ang)
    n = positions.shape[0]
    rest = HEAD_DIM - ROT_DIM
    comp_c = jnp.concatenate([cos, cos, jnp.ones((n, rest), F32)], axis=1)
    comp_a = jnp.concatenate([-sin, jnp.zeros((n, half + rest), F32)], axis=1)
    comp_b = jnp.concatenate([jnp.zeros((n, half), F32), sin, jnp.zeros((n, rest), F32)], axis=1)
    return tuple(jnp.tile(c, (1, LANES // HEAD_DIM)) for c in (comp_c, comp_a, comp_b))


def _block_diag(blocks):
    g, a, b = blocks.shape
    tiled = jnp.tile(blocks.reshape(g * a, b), (1, g))
    rg = lax.broadcasted_iota(jnp.int32, (g * a, g * b), 0) // a
    cg = lax.broadcasted_iota(jnp.int32, (g * a, g * b), 1) // b
    return jnp.where(rg == cg, tiled, jnp.zeros_like(tiled))


def kernel(x_prompt, x_sample, cache_kb, cache_vb, cache_kc, cache_vc, state_pool, page_table, norm1_g, w_in, pool_w, pool_scale, qn_g, kn_g, lam_q1, lam_k1, lam_q2, lam_k2, subln_g, w_proj_a, w_proj_b, w_proj_c, w_out, norm2_g, w_group, b_group, w_router, b_router, w_e_gate, w_e_up, w_e_down):
    bp, tp, d = x_prompt.shape
    bs, ts, _ = x_sample.shape
    depth = w_in.shape[0]
    n_pages, page = page_table.shape[1], cache_kb.shape[2]
    past_len = n_pages * page
    pool_width = pool_w.shape[1] * pool_w.shape[2]
    diff_heads, stick_heads = cache_kb.shape[3], cache_kc.shape[3]
    diff_w = diff_heads * 2 * HEAD_DIM
    stick_w = stick_heads * HEAD_DIM
    sizes = (pool_width, diff_w, diff_w, diff_w, stick_w, stick_w, stick_w, d, d, d)

    tm_p = min(512, tp)
    tm_s = min(512, bs * ts)
    tq = min(256, tp)
    tk = min(512, tp)
    tm_moe_p = min(1024, bp * tp)
    tm_moe_s = min(1024, bs * ts)
    tabs_p = _rope_tables(jnp.arange(tp))
    tabs_s = _rope_tables(past_len + (jnp.arange(tm_s) % ts))
    pos_block_p = lambda i: i % (tp // tm_p)
    pos_block_s = lambda i: 0

    gm = _block_diag(jnp.full((diff_w // HEAD_DIM, HEAD_DIM, HEAD_DIM), 1.0 / HEAD_DIM, F32)).astype(BF16)
    n_pool_pages = cache_kb.shape[1]
    ckb, cvb = (c.reshape(depth, n_pool_pages, page * diff_heads, 2 * HEAD_DIM) for c in (cache_kb, cache_vb))
    ckc, cvc = (c.transpose(0, 1, 3, 4, 2).reshape(depth, n_pool_pages, stick_w, page) for c in (cache_kc, cache_vc))
    zero_pool = jnp.zeros((bp, POOL_STATE, pool_width), F32)

    xp = x_prompt.reshape(bp * tp, d)
    xs = x_sample.reshape(bs * ts, d)
    pool_rows_p, rows_s = [], []
    cache_p = ()
    w_in_b, wa, wb, wc, wo, w_g, w_u = (w.astype(BF16) for w in (w_in, w_proj_a, w_proj_b, w_proj_c, w_out,
                                                                 w_e_gate, w_e_up))
    w_d = w_e_down.reshape(depth, N_GROUPS, -1, d).astype(BF16)
    for l in range(depth):
        lam_init = 0.8 - 0.6 * math.exp(-0.3 * l)
        g1 = norm1_g[l].reshape(1, d)
        qg = jnp.tile(qn_g[l].reshape(1, -1), (1, diff_heads))
        kg = jnp.tile(kn_g[l].reshape(1, -1), (1, diff_heads))
        w_bd = _block_diag(pool_w[l]).astype(BF16)
        p_scale = pool_scale[l].reshape(1, -1)
        lam_vecs = tuple(v[l].reshape(1, -1) for v in (lam_q1, lam_k1, lam_q2, lam_k2))
        sg = subln_g[l].reshape(1, -1)
        g2 = norm2_g[l].reshape(1, d)
        wr = jnp.concatenate([w_group[l], w_router[l],
                              jnp.zeros((d, LANES - N_GROUPS - N_EXPERTS), F32)], axis=1)
        wr_hi = wr.astype(BF16)
        wr_lo = (wr - wr_hi.astype(F32)).astype(BF16)
        br = jnp.concatenate([b_group[l], b_router[l],
                              jnp.zeros((LANES - N_GROUPS - N_EXPERTS,), F32)]).reshape(1, LANES)

        def tail(x, o_a, o_b, o_c, gates, tm, tm_moe):
            x_mid, h2, comb = _merge(x, o_a, o_b, o_c, gates, wa, wb, wc, wo, g2, wr_hi, wr_lo, br, tm, l)
            return _moe(x_mid, h2, comb, w_g, w_u, w_d, tm_moe, l)

        a_in, qb, kb, vb, qc, kc, vc, gates = _in_proj(
            xp, g1, w_in_b, qg, kg, gm, tabs_p, pos_block_p, sizes, tm_p, tp, True, l, depth, cache_p)
        cache_p = (kb, vb, kc, vc)
        o_a = _pool(a_in, zero_pool, w_bd, p_scale, bp, tp, 0)
        o_b = _diff_attn_prompt(qb, kb, vb, lam_vecs, sg, bp, tp, lam_init, tq, tk, 4, l)
        o_c = _stick_prompt(qc, kc, vc, bp, tp, tq, l)
        xp = tail(xp, o_a, o_b, o_c, gates, tm_p, tm_moe_p)
        pool_rows_p.append(a_in.reshape(bp, tp, -1)[:, tp - POOL_STATE:])

        a_in, qb, kb, vb, qc, kc, vc, gates = _in_proj(
            xs, g1, w_in_b, qg, kg, gm, tabs_s, pos_block_s, sizes, tm_s, ts, False, l)
        o_a = _pool(a_in, state_pool[l], w_bd, p_scale, bs, ts, past_len)
        o_b = _diff_decode(qb, kb, vb, ckb, cvb, page_table, l, lam_vecs, sg, ts, lam_init)
        o_c = _stick_decode(qc, kc, vc, ckc, cvc, page_table, l, ts)
        xs = tail(xs, o_a, o_b, o_c, gates, tm_s, tm_moe_s)
        new_pool = jnp.concatenate([state_pool[l], a_in.reshape(bs, ts, -1)], axis=1)[:, -POOL_STATE:]
        rows_s.append((kb, vb, kc, vc, new_pool))

    kb_p, vb_p, kc_p, vc_p = cache_p
    pool_p = jnp.stack(pool_rows_p)
    kb_s, vb_s, kc_s, vc_s, pool_s = [jnp.stack(z) for z in zip(*rows_s)]
    diff_shape = lambda b, t: (depth, b, t, diff_heads, 2 * HEAD_DIM)
    stick_t = lambda z: z.reshape(depth, bp, stick_heads, HEAD_DIM, tp).transpose(0, 1, 4, 2, 3)
    return (xp.reshape(bp, tp, d), xs.reshape(bs, ts, d),
            kb_p.reshape(diff_shape(bp, tp)), vb_p.reshape(diff_shape(bp, tp)), stick_t(kc_p), stick_t(vc_p), pool_p,
            kb_s.reshape(diff_shape(bs, ts)), vb_s.reshape(diff_shape(bs, ts)),
            kc_s.reshape(depth, bs, ts, stick_heads, HEAD_DIM), vc_s.reshape(depth, bs, ts, stick_heads, HEAD_DIM),
            pool_s)
```

```python
import functools
import math

import numpy as np
import jax
import jax.numpy as jnp
from jax import lax
from jax.experimental import pallas as pl
from jax.experimental.pallas import tpu as pltpu

F32 = jnp.float32
BF16 = jnp.bfloat16

HEAD_DIM = 64
ROT_DIM = HEAD_DIM // 4
ROPE_THETA = 500000.0
POOL_WINDOWS = (2, 4, 8, 16)
POOL_STATE = max(POOL_WINDOWS) - 1
N_GROUPS = 4
EXPERTS_PER_GROUP = 4
N_EXPERTS = N_GROUPS * EXPERTS_PER_GROUP
EPS = 1e-6
LANES = 128
SUBLANES = 8
ROUTE_OFF = N_GROUPS
VMEM_LIMIT = 56 * 1024 * 1024
NEG = -1e30
LOG2E = math.log2(math.e)


def _dot(a, b):
    return jnp.dot(a, b, preferred_element_type=F32)


def _dot_nt(a, b):
    return lax.dot_general(a, b, (((1,), (1,)), ((), ())), preferred_element_type=F32)


def _params(sem):
    return pltpu.CompilerParams(dimension_semantics=sem, vmem_limit_bytes=VMEM_LIMIT)


def _layer_spec(stacked, layer):
    tail = stacked.shape[1:]
    return pl.BlockSpec((None,) + tail, lambda *_: (layer,) + (0,) * len(tail), pipeline_mode=pl.Buffered(1))


def _const_spec(shape):
    return pl.BlockSpec(shape, lambda *_: (0,) * len(shape), pipeline_mode=pl.Buffered(1))


def _in_proj_kernel(x_ref, g_ref, w_ref, qg_ref, kg_ref, gm_ref, c_ref, sa_ref, sb_ref, *rest,
                    sizes, cache_layout, n_prev):
    a_out, q_out, k_out, v_out, qc_out, kc_out, vc_out, gate_out = rest[n_prev:]
    x = x_ref[...]
    tm = x.shape[0]
    ms = jnp.mean(x * x, axis=-1, keepdims=True)
    h = ((x * lax.rsqrt(ms + EPS)) * g_ref[...]).astype(BF16)
    offs = np.concatenate([[0], np.cumsum(sizes)]).tolist()

    def seg(i, j=None):
        j = i if j is None else j
        return _dot(h, w_ref[:, offs[i]:offs[j + 1]])

    qk_w = sizes[1]
    reps = qk_w // LANES
    cos_t = jnp.tile(c_ref[...], (1, reps))
    sin_a = jnp.tile(sa_ref[...], (1, reps))
    sin_b = jnp.tile(sb_ref[...], (1, reps))
    half = ROT_DIM // 2

    def qk_norm_rope(y, gain):
        gms = _dot((y * y).astype(BF16), gm_ref[...])
        yn = (y * lax.rsqrt(gms + EPS)) * gain
        return (yn * cos_t + pltpu.roll(yn, qk_w - half, 1) * sin_a
                + pltpu.roll(yn, half, 1) * sin_b)

    def put_rows(ref, val):
        if cache_layout:
            for hd in range(reps):
                ref[pl.ds(hd, tm, stride=reps), :] = val[:, hd * LANES:(hd + 1) * LANES]
        else:
            ref[...] = val

    def put_cols(ref, val):
        ref[...] = val.T if cache_layout else val

    scale = HEAD_DIM ** -0.5
    a_out[...] = seg(0)
    q_out[...] = (qk_norm_rope(seg(1), qg_ref[...]) * (scale * LOG2E)).astype(q_out.dtype)
    put_rows(k_out, qk_norm_rope(seg(2), kg_ref[...]))
    put_rows(v_out, seg(3))
    qc_out[...] = (seg(4) * scale).astype(qc_out.dtype)
    put_cols(kc_out, seg(5))
    put_cols(vc_out, seg(6))
    gate_out[...] = jax.nn.sigmoid(seg(7, 9)).astype(gate_out.dtype)


def _in_proj(x, g1, w_in, qg, kg, gm, rope_tabs, pos_block, sizes, tm, seq_len, cache_layout,
             layer=0, depth=1, prev=()):
    n, d = x.shape
    cos_t, sin_a, sin_b = rope_tabs
    row = lambda i: (i, 0)
    tab = pl.BlockSpec((tm, LANES), lambda i: (pos_block(i), 0))
    diff_w, stick_w = sizes[1], sizes[4]
    heads = diff_w // LANES
    tiles = n // tm

    def plain(w, dt):
        return pl.BlockSpec((tm, w), row), jax.ShapeDtypeStruct((n, w), dt)

    if cache_layout:
        per_seq = seq_len // tm
        n_seq = n // seq_len
        rows = (pl.BlockSpec((tm * heads, LANES), lambda i: (layer * tiles + i, 0)),
                jax.ShapeDtypeStruct((depth * n * heads, LANES), F32))
        cols = (pl.BlockSpec((stick_w, tm), lambda i: (layer * n_seq + i // per_seq, i % per_seq)),
                jax.ShapeDtypeStruct((depth * n_seq * stick_w, seq_len), F32))
    else:
        rows, cols = plain(diff_w, F32), plain(stick_w, F32)
    outs = [plain(sizes[0], F32), plain(diff_w, BF16), rows, rows,
            plain(stick_w, BF16), cols, cols, plain(sum(sizes[7:]), BF16)]
    ins = [x, g1, w_in, qg, kg, gm, cos_t, sin_a, sin_b]
    cache_outs = (2, 3, 5, 6)
    return pl.pallas_call(
        functools.partial(_in_proj_kernel, sizes=tuple(sizes), cache_layout=cache_layout, n_prev=len(prev)),
        grid=(tiles,),
        in_specs=[pl.BlockSpec((tm, d), row), _const_spec((1, d)), _layer_spec(w_in, layer),
                  _const_spec(qg.shape), _const_spec(kg.shape), _const_spec(gm.shape), tab, tab, tab]
        + [pl.BlockSpec(memory_space=pl.ANY)] * len(prev),
        out_specs=[o[0] for o in outs],
        out_shape=[o[1] for o in outs],
        input_output_aliases={len(ins) + j: cache_outs[j] for j in range(len(prev))},
        compiler_params=_params(("parallel",)),
        name="in_proj",
    )(*ins, *prev)


def _pool_kernel(a_ref, pre_ref, w_ref, s_ref, o_ref, u_scr, *, pos0, t, chunk):
    nb, width = pre_ref.shape[0], a_ref.shape[1]
    pad = POOL_STATE + 1
    levels = len(POOL_WINDOWS)
    assert POOL_WINDOWS == tuple(2 ** (k + 1) for k in range(levels))
    group = width // levels
    lane_g = lax.broadcasted_iota(jnp.int32, (1, width), 1) // group
    for b in range(nb):
        u_scr[0, 0:1, :] = jnp.zeros((1, width), F32)
        u_scr[0, 1:pad, :] = pre_ref[b]
        u_scr[0, pad:pad + t, :] = a_ref[b * t:(b + 1) * t, :]
        for k in range(1, levels):
            w, lo = 2 ** (k - 1), 2 ** k - 1
            u_scr[k, lo:pad, :] = u_scr[k - 1, lo:pad, :] + u_scr[k - 1, lo - w:pad - w, :]
        for c0 in range(0, t, chunk):
            r0, r1 = pad + c0, pad + c0 + chunk
            pos = pos0 + c0 + lax.broadcasted_iota(jnp.int32, (chunk, 1), 0)
            a = u_scr[0, r0:r1, :]
            pooled = jnp.zeros_like(a)
            for k in range(1, levels + 1):
                w = 2 ** (k - 1)
                s = u_scr[k - 1, r0:r1, :] + u_scr[k - 1, r0 - w:r1 - w, :]
                if k < levels:
                    u_scr[k, r0:r1, :] = s
                cnt = jnp.minimum(2 * w, pos + 1).astype(F32)
                pooled = jnp.where(lane_g == k - 1, s / cnt, pooled)
            pooled = pooled - a
            o_ref[b * t + c0:b * t + c0 + chunk, :] = (
                _dot(pooled.astype(BF16), w_ref[...]) * s_ref[...]).astype(o_ref.dtype)


def _pool(a_in, prefix, w_bd, scale, b, t, pos0):
    width = a_in.shape[1]
    chunk = min(t, 256)
    nb = max(1, min(b, LANES // t))
    return pl.pallas_call(
        functools.partial(_pool_kernel, pos0=pos0, t=t, chunk=chunk),
        grid=(b // nb,),
        in_specs=[pl.BlockSpec((nb * t, width), lambda i: (i, 0)),
                  pl.BlockSpec((nb, POOL_STATE, width), lambda i: (i, 0, 0)),
                  _const_spec(w_bd.shape), _const_spec(scale.shape)],
        out_specs=pl.BlockSpec((nb * t, width), lambda i: (i, 0)),
        out_shape=jax.ShapeDtypeStruct(a_in.shape, BF16),
        scratch_shapes=[pltpu.VMEM((len(POOL_WINDOWS), t + POOL_STATE + 1, width), F32)],
        compiler_params=_params(("parallel",)),
        name="pool",
    )(a_in, prefix, w_bd, scale)


def _lambda(lq1, lk1, lq2, lk2, lam_init):
    s1 = jnp.sum(lq1[...] * lk1[...], axis=-1, keepdims=True)
    s2 = jnp.sum(lq2[...] * lk2[...], axis=-1, keepdims=True)
    return jnp.exp(s1) - jnp.exp(s2) + lam_init


def _sub_norm(o, gain, lam_init):
    ms = jnp.mean(o * o, axis=-1, keepdims=True)
    return (o * lax.rsqrt(ms + EPS)) * gain * (1.0 - lam_init)


def _softplus(z):
    return jnp.maximum(z, 0.0) + jnp.log(1.0 + jnp.exp(-jnp.abs(z)))


def _split_bf16(x):
    hi = x.astype(BF16)
    lo = (x - hi.astype(F32)).astype(BF16)
    return hi, lo


def _pad_rows(x, rows):
    return jnp.concatenate([x, jnp.zeros((rows - x.shape[0], x.shape[1]), x.dtype)], axis=0)


def _lane_chunks(x):
    return [x[:, i * LANES:(i + 1) * LANES] for i in range(x.shape[1] // LANES)]


def _diff_attn_kernel(q_ref, k_ref, v_ref, lq1, lk1, lq2, lk2, sg_ref, o_ref, s_scr, *, tq, tk, heads, lam_init):
    hp = pl.program_id(1)
    qi = pl.program_id(2)
    hw = 2 * HEAD_DIM
    hpb = q_ref.shape[1] // hw
    lane = lax.broadcasted_iota(jnp.int32, (1, hw), 1)
    qs = []
    for j in range(hpb):
        q = q_ref[:, j * hw:(j + 1) * hw]
        zero = jnp.zeros_like(q)
        qs += [jnp.where(lane < HEAD_DIM, q, zero), jnp.where(lane >= HEAD_DIM, q, zero)]
    streams = range(2 * hpb)
    n_blk = (qi * tq + tq + tk - 1) // tk
    last = n_blk - 1

    def rows(ref, kb, j):
        start = pl.multiple_of(kb * (tk * heads), tk * heads) + hp * hpb + j
        return ref[pl.ds(start, tk, stride=heads), :].astype(BF16)

    def scores(kb, mx, masked):
        ks = [rows(k_ref, kb, j) for j in range(hpb)]
        col0 = pl.multiple_of(kb * tk, tk)
        s = [_dot_nt(qs[i], ks[i // 2]) for i in streams]
        if masked:
            row = qi * tq + lax.broadcasted_iota(jnp.int32, (tq, tk), 0)
            col = col0 + lax.broadcasted_iota(jnp.int32, (tq, tk), 1)
            s = [jnp.where(col <= row, s[i], NEG) for i in streams]
        for i in streams:
            s_scr[i, :, pl.ds(col0, tk)] = s[i]
        return tuple(functools.reduce(jnp.maximum, _lane_chunks(s[i]), mx[i]) for i in streams)

    neg = jnp.full((tq, LANES), NEG, F32)
    mx = lax.fori_loop(0, last, lambda kb, m: scores(kb, m, False), (neg,) * len(streams))
    mx = scores(last, mx, True)
    m_b = [jnp.broadcast_to(jnp.max(m, axis=-1, keepdims=True), (tq, LANES)) for m in mx]

    def accumulate(kb, carry):
        vs = [rows(v_ref, kb, j) for j in range(hpb)]
        col0 = pl.multiple_of(kb * tk, tk)
        p = [[jnp.exp2(sc - m_b[i]) for sc in _lane_chunks(s_scr[i, :, pl.ds(col0, tk)])] for i in streams]
        den = [functools.reduce(jnp.add, p[i], carry[i][0]) for i in streams]
        acc = [carry[i][1] + _dot(jnp.concatenate(p[i], axis=1).astype(BF16), vs[i // 2]) for i in streams]
        return tuple((den[i], acc[i]) for i in streams)

    zero_c = (jnp.zeros((tq, LANES), F32), jnp.zeros((tq, hw), F32))
    res = lax.fori_loop(0, n_blk, accumulate, (zero_c,) * len(streams))
    lam = _lambda(lq1, lk1, lq2, lk2, lam_init)
    norm = [acc / jnp.sum(den, axis=-1, keepdims=True) for den, acc in res]
    for j in range(hpb):
        o = norm[2 * j] - lam * norm[2 * j + 1]
        o_ref[:, j * hw:(j + 1) * hw] = _sub_norm(o, sg_ref[...], lam_init).astype(o_ref.dtype)


def _diff_attn_prompt(q, k, v, lam_vecs, sg, b, t, lam_init, tq, tk, hpb, layer):
    n, w = q.shape
    hw = 2 * HEAD_DIM
    heads = w // hw
    nq = t // tq
    vec = _const_spec((1, HEAD_DIM))
    kv = pl.BlockSpec((t * heads, hw), lambda bi, h, i: (layer * b + bi, 0))
    return pl.pallas_call(
        functools.partial(_diff_attn_kernel, tq=tq, tk=tk, heads=heads, lam_init=lam_init),
        grid=(b, heads // hpb, nq),
        in_specs=[pl.BlockSpec((tq, hpb * hw), lambda bi, h, i: (bi * nq + i, h)), kv, kv,
                  vec, vec, vec, vec, _const_spec((1, hw))],
        out_specs=pl.BlockSpec((tq, hpb * hw), lambda bi, h, i: (bi * nq + i, h)),
        out_shape=jax.ShapeDtypeStruct((n, w), BF16),
        scratch_shapes=[pltpu.VMEM((2 * hpb, tq, t), F32)],
        compiler_params=_params(("parallel", "arbitrary", "arbitrary")),
        name="diff_attn_prompt",
    )(q, k, v, *lam_vecs, sg)


def _stick_kernel(q_ref, k_ref, v_ref, o_ref, *, tq):
    qi = pl.program_id(1)
    q = q_ref[...]
    w = q.shape[1]
    lane_h = lax.broadcasted_iota(jnp.int32, (1, w), 1) // HEAD_DIM
    row = lax.broadcasted_iota(jnp.int32, (tq, tq), 0)
    col = lax.broadcasted_iota(jnp.int32, (tq, tq), 1)
    tri = jnp.where(row > col, 1.0, 0.0).astype(BF16)
    heads = w // HEAD_DIM
    qh = [jnp.where(lane_h == hh, q, jnp.zeros_like(q)) for hh in range(heads)]

    def step(kb, carry, diag):
        start = pl.multiple_of(kb * tq, tq)
        k = k_ref[:, pl.ds(start, tq)].astype(BF16)
        v = v_ref[:, pl.ds(start, tq)].astype(BF16)
        hs = range(heads)
        z = [_dot(qh[hh], k) for hh in hs]
        sp = [_softplus(z[hh]) for hh in hs]
        lk = [-sp[hh] for hh in hs]
        if diag:
            lk = [jnp.where(col < row, lk[hh], 0.0) for hh in hs]
        later = [_dot(lk[hh].astype(BF16), tri) + carry[hh][1] for hh in hs]
        a = [jnp.exp(z[hh] - sp[hh] + later[hh]) for hh in hs]
        if diag:
            a = [jnp.where(col < row, a[hh], 0.0) for hh in hs]
        acc = [carry[hh][0] + _dot_nt(a[hh].astype(BF16), v[hh * HEAD_DIM:(hh + 1) * HEAD_DIM, :]) for hh in hs]
        run = [carry[hh][1] + jnp.sum(lk[hh], axis=-1, keepdims=True) for hh in hs]
        return tuple((acc[hh], run[hh]) for hh in hs)

    init = tuple((jnp.zeros((tq, HEAD_DIM), F32), jnp.zeros((tq, 1), F32)) for _ in range(heads))
    carry = step(qi, init, True)
    carry = lax.fori_loop(0, qi, lambda j, c: step(qi - 1 - j, c, False), carry)
    o_ref[...] = jnp.concatenate([c[0] for c in carry], axis=1).astype(o_ref.dtype)


def _stick_prompt(q, k_t, v_t, b, t, tq, layer):
    n, w = q.shape
    nq = t // tq
    kv = pl.BlockSpec((w, t), lambda bi, i: (layer * b + bi, 0))
    return pl.pallas_call(
        functools.partial(_stick_kernel, tq=tq),
        grid=(b, nq),
        in_specs=[pl.BlockSpec((tq, w), lambda bi, i: (bi * nq + i, 0)), kv, kv],
        out_specs=pl.BlockSpec((tq, w), lambda bi, i: (bi * nq + i, 0)),
        out_shape=jax.ShapeDtypeStruct((n, w), BF16),
        compiler_params=_params(("parallel", "arbitrary")),
        name="stick_prompt",
    )(q, k_t, v_t)


def _head_rows(x, heads):
    return jnp.concatenate([x[:, h * LANES:(h + 1) * LANES] for h in range(heads)], axis=0)


def _diff_decode_kernel(pt_ref, q_ref, kn_ref, vn_ref, *rest, n_pages, lam_init):
    del pt_ref
    kp = rest[:n_pages]
    vp = rest[n_pages:2 * n_pages]
    lq1, lk1, lq2, lk2, sg_ref, o_ref, s_scr, kh_scr, vh_scr = rest[2 * n_pages:]
    t, w = q_ref.shape
    hw = 2 * HEAD_DIM
    heads = w // hw
    page = kp[0].shape[0] // heads
    past = n_pages * page
    rows = 2 * t
    for h in range(heads):
        for j in range(n_pages):
            kh_scr[h, j * page:(j + 1) * page, :] = kp[j][pl.ds(h, page, stride=heads), :].astype(BF16)
            vh_scr[h, j * page:(j + 1) * page, :] = vp[j][pl.ds(h, page, stride=heads), :].astype(BF16)
        kh_scr[h, past:, :] = _pad_rows(kn_ref[:, h * hw:(h + 1) * hw], page).astype(BF16)
        vh_scr[h, past:, :] = _pad_rows(vn_ref[:, h * hw:(h + 1) * hw], page).astype(BF16)
    q = q_ref[...].astype(F32)
    lane = lax.broadcasted_iota(jnp.int32, (1, hw), 1)
    r = lax.broadcasted_iota(jnp.int32, (rows, page), 0)
    c = lax.broadcasted_iota(jnp.int32, (rows, page), 1)
    new_mask = c <= r % t
    for h in range(heads):
        qh = q[:, h * hw:(h + 1) * hw]
        qrows = jnp.concatenate([jnp.where(lane < HEAD_DIM, qh, 0.0), jnp.where(lane >= HEAD_DIM, qh, 0.0)],
                                axis=0).astype(BF16)
        s = _dot_nt(qrows, kh_scr[h])
        s_scr[h * rows:(h + 1) * rows, :past] = s[:, :past]
        s_scr[h * rows:(h + 1) * rows, past:] = jnp.where(new_mask, s[:, past:], NEG)
    m = jnp.max(s_scr[...], axis=-1, keepdims=True)
    p = jnp.exp2(s_scr[...] - m)
    den = jnp.sum(p, axis=-1, keepdims=True)
    p = p.astype(BF16)
    lam = _lambda(lq1, lk1, lq2, lk2, lam_init)
    for h in range(heads):
        o = _dot(p[h * rows:(h + 1) * rows], vh_scr[h]) / den[h * rows:(h + 1) * rows]
        oh = o[:t] - lam * o[t:]
        o_ref[:, h * hw:(h + 1) * hw] = _sub_norm(oh, sg_ref[...], lam_init).astype(o_ref.dtype)


def _page_specs(cache, layer, n_pages):
    blk = (None, None) + cache.shape[2:]
    return [pl.BlockSpec(blk, lambda b, pt, j=j: (layer, pt[b, j], 0, 0)) for j in range(n_pages)]


def _diff_decode(q, k_new, v_new, cache_k, cache_v, page_table, layer, lam_vecs, sg, t, lam_init):
    n, w = q.shape
    b, n_pages = page_table.shape
    prow = cache_k.shape[2]
    heads = w // (2 * HEAD_DIM)
    tok = pl.BlockSpec((t, w), lambda bi, pt: (bi, 0))
    vec = pl.BlockSpec((1, HEAD_DIM), lambda bi, pt: (0, 0))
    grid_spec = pltpu.PrefetchScalarGridSpec(
        num_scalar_prefetch=1,
        grid=(b,),
        in_specs=[tok, tok, tok] + _page_specs(cache_k, layer, n_pages) + _page_specs(cache_v, layer, n_pages)
        + [vec, vec, vec, vec, pl.BlockSpec((1, 2 * HEAD_DIM), lambda bi, pt: (0, 0))],
        out_specs=tok,
        scratch_shapes=[pltpu.VMEM((2 * heads * t, (n_pages + 1) * (prow // heads)), F32),
                        pltpu.VMEM((heads, (n_pages + 1) * (prow // heads), 2 * HEAD_DIM), BF16),
                        pltpu.VMEM((heads, (n_pages + 1) * (prow // heads), 2 * HEAD_DIM), BF16)],
    )
    return pl.pallas_call(
        functools.partial(_diff_decode_kernel, n_pages=n_pages, lam_init=lam_init),
        grid_spec=grid_spec,
        out_shape=jax.ShapeDtypeStruct((n, w), BF16),
        compiler_params=_params(("parallel",)),
        name="diff_decode",
    )(page_table, q, k_new, v_new, *([cache_k] * n_pages), *([cache_v] * n_pages), *lam_vecs, sg)


def _stick_decode_kernel(pt_ref, q_ref, kn_ref, vn_ref, *rest, n_pages):
    del pt_ref
    kp = rest[:n_pages]
    vp = rest[n_pages:2 * n_pages]
    o_ref = rest[2 * n_pages]
    t, w = q_ref.shape
    heads = w // HEAD_DIM
    page = kp[0].shape[1]
    nq = heads * t
    q = q_ref[...].astype(F32)
    lane_h = lax.broadcasted_iota(jnp.int32, (1, w), 1) // HEAD_DIM
    qrows = jnp.concatenate([jnp.where(lane_h == h, q, 0.0) for h in range(heads)], axis=0).astype(BF16)
    r = lax.broadcasted_iota(jnp.int32, (page, page), 0)
    c = lax.broadcasted_iota(jnp.int32, (page, page), 1)
    tri = jnp.where(r > c, 1.0, 0.0).astype(BF16)
    qr = lax.broadcasted_iota(jnp.int32, (nq, page), 0) % t
    kc = lax.broadcasted_iota(jnp.int32, (nq, page), 1)
    new_mask = kc < qr
    pages = range(n_pages + 1)
    z = [_dot(qrows, kp[j][...].astype(BF16)) for j in range(n_pages)]
    z.append(_dot_nt(qrows, _pad_rows(kn_ref[...], page).astype(BF16)))
    sp = [_softplus(z[j]) for j in pages]
    lk = [-sp[j] for j in pages]
    lk[n_pages] = jnp.where(new_mask, lk[n_pages], 0.0)
    within = [_dot(lk[j].astype(BF16), tri) for j in pages]
    totals = [jnp.sum(lk[j], axis=-1, keepdims=True) for j in pages]
    run = jnp.zeros((nq, 1), F32)
    a = [None] * (n_pages + 1)
    for j in range(n_pages, -1, -1):
        a[j] = jnp.exp(z[j] - sp[j] + within[j] + run)
        run = run + totals[j]
    a[n_pages] = jnp.where(new_mask, a[n_pages], 0.0)
    acc = _dot(a[n_pages].astype(BF16), _pad_rows(vn_ref[...], page).astype(BF16))
    for j in range(n_pages):
        acc = acc + _dot_nt(a[j].astype(BF16), vp[j][...].astype(BF16))
    o = jnp.zeros((t, w), F32)
    for h in range(heads):
        o = jnp.where(lane_h == h, acc[h * t:(h + 1) * t], o)
    o_ref[...] = o.astype(o_ref.dtype)


def _stick_decode(q, k_new, v_new, cache_k, cache_v, page_table, layer, t):
    n, w = q.shape
    b, n_pages = page_table.shape
    tok = pl.BlockSpec((t, w), lambda bi, pt: (bi, 0))
    grid_spec = pltpu.PrefetchScalarGridSpec(
        num_scalar_prefetch=1,
        grid=(b,),
        in_specs=[tok, tok, tok] + _page_specs(cache_k, layer, n_pages) + _page_specs(cache_v, layer, n_pages),
        out_specs=tok,
    )
    return pl.pallas_call(
        functools.partial(_stick_decode_kernel, n_pages=n_pages),
        grid_spec=grid_spec,
        out_shape=jax.ShapeDtypeStruct((n, w), BF16),
        compiler_params=_params(("parallel",)),
        name="stick_decode",
    )(page_table, q, k_new, v_new, *([cache_k] * n_pages), *([cache_v] * n_pages))


def _merge_kernel(x_ref, oa_ref, ob_ref, oc_ref, gate_ref, wa_ref, wb_ref, wc_ref, wo_ref,
                  g2_ref, wr_hi_ref, wr_lo_ref, br_ref, x_out, h_out, comb_out):
    d = x_ref.shape[1]
    gate = gate_ref[...].astype(F32)
    merged = (gate[:, 0:d] * _dot(oa_ref[...], wa_ref[...])
              + gate[:, d:2 * d] * _dot(ob_ref[...], wb_ref[...])
              + gate[:, 2 * d:3 * d] * _dot(oc_ref[...], wc_ref[...]))
    x = x_ref[...] + _dot(merged.astype(BF16), wo_ref[...])
    x_out[...] = x
    ms = jnp.mean(x * x, axis=-1, keepdims=True)
    h = (x * lax.rsqrt(ms + EPS)) * g2_ref[...]
    h_out[...] = h.astype(h_out.dtype)
    h_hi, h_lo = _split_bf16(h)
    logits = (_dot(h_hi, wr_hi_ref[...]) + _dot(h_lo, wr_hi_ref[...]) + _dot(h_hi, wr_lo_ref[...])
              + br_ref[...])
    lane = lax.broadcasted_iota(jnp.int32, logits.shape, 1).astype(F32)
    gl = jnp.where(lane < N_GROUPS, logits, NEG)
    gmax = jnp.max(gl, axis=-1, keepdims=True)
    gidx = jnp.min(jnp.where(gl == gmax, lane, 1e9), axis=-1, keepdims=True)
    gprob = 1.0 / jnp.sum(jnp.where(lane < N_GROUPS, jnp.exp(logits - gmax), 0.0), axis=-1, keepdims=True)
    rel = lane - ROUTE_OFF - EXPERTS_PER_GROUP * gidx
    el = jnp.where((rel >= 0) & (rel < EXPERTS_PER_GROUP), logits, NEG)
    t1 = jnp.max(el, axis=-1, keepdims=True)
    i1 = jnp.min(jnp.where(el == t1, lane, 1e9), axis=-1, keepdims=True)
    el2 = jnp.where(lane == i1, NEG, el)
    t2 = jnp.max(el2, axis=-1, keepdims=True)
    i2 = jnp.min(jnp.where(el2 == t2, lane, 1e9), axis=-1, keepdims=True)
    e21 = jnp.exp(t2 - t1)
    w1 = gprob / (1.0 + e21)
    w2 = w1 * e21
    comb_out[...] = jnp.where(lane == 0, gidx, jnp.where(lane == i1, w1, jnp.where(lane == i2, w2, 0.0)))


def _merge(x, o_a, o_b, o_c, gates, wa, wb, wc, wo, g2, wr_hi, wr_lo, br, tm, layer):
    n, d = x.shape
    row = lambda i: (i, 0)
    ins = [x, o_a, o_b, o_c, gates]
    stacks = [wa, wb, wc, wo]
    consts = [g2, wr_hi, wr_lo, br]
    return pl.pallas_call(
        _merge_kernel,
        grid=(n // tm,),
        in_specs=[pl.BlockSpec((tm, a.shape[1]), row) for a in ins] + [_layer_spec(s, layer) for s in stacks]
        + [_const_spec(c.shape) for c in consts],
        out_specs=[pl.BlockSpec((tm, d), row), pl.BlockSpec((tm, d), row), pl.BlockSpec((tm, LANES), row)],
        out_shape=[jax.ShapeDtypeStruct((n, d), F32), jax.ShapeDtypeStruct((n, d), BF16),
                   jax.ShapeDtypeStruct((n, LANES), F32)],
        compiler_params=_params(("parallel",)),
        name="merge",
    )(*ins, *stacks, *consts)


def _dispatch_plan(comb, tm):
    n = comb.shape[0]
    nt = n // tm
    g = comb[:, 0].astype(jnp.int32).reshape(nt, tm)
    onehot = (g[:, :, None] == jnp.arange(N_GROUPS, dtype=jnp.int32)[None, None, :]).astype(jnp.int32)
    before = (lax.broadcasted_iota(jnp.int32, (tm, tm), 1) < lax.broadcasted_iota(jnp.int32, (tm, tm), 0))
    rank = jnp.einsum("ij,tjg->tig", before.astype(F32), onehot.astype(F32)).astype(jnp.int32)
    counts = jnp.sum(onehot, axis=1)
    padded = (counts + SUBLANES - 1) // SUBLANES * SUBLANES
    gstart = jnp.cumsum(padded, axis=1) - padded
    dest = jnp.sum((gstart[:, None, :] + rank) * onehot, axis=2)
    return dest.reshape(n), gstart.reshape(-1), counts.reshape(-1)


def _moe_kernel(dest_ref, gstart_ref, count_ref, x_ref, h_ref, comb_ref, wg_ref, wu_ref, wd_ref, o_ref,
                hf, hs, cs, ys, *, tm, win):
    i = pl.program_id(0)
    g = pl.program_id(1)
    base = i * tm

    @pl.when(g == 0)
    def _sort():
        hf[...] = h_ref[...].astype(F32)
        hs[...] = jnp.zeros_like(hs)
        cs[...] = jnp.zeros_like(cs)
        ys[...] = jnp.zeros_like(ys)

        def body(r, carry):
            d = dest_ref[base + r]
            hs[pl.ds(d, 1), :] = hf[pl.ds(r, 1), :]
            cs[pl.ds(d, 1), :] = comb_ref[pl.ds(r, 1), :]
            return carry

        lax.fori_loop(0, tm, body, 0, unroll=8)

    gs = gstart_ref[i * N_GROUPS + g]
    n_win = (count_ref[i * N_GROUPS + g] + win - 1) // win
    lane = lax.broadcasted_iota(jnp.int32, (1, LANES), 1)

    def window(w, carry):
        r0 = pl.multiple_of(gs + w * win, SUBLANES)
        hw = hs[pl.ds(r0, win), :].astype(BF16)
        cw = cs[pl.ds(r0, win), :]
        experts = range(EXPERTS_PER_GROUP)
        gate = [_dot(hw, wg_ref[e]) for e in experts]
        up = [_dot(hw, wu_ref[e]) for e in experts]
        ce = [jnp.sum(jnp.where(lane == ROUTE_OFF + EXPERTS_PER_GROUP * g + e, cw, 0.0), axis=-1, keepdims=True)
              for e in experts]
        act = jnp.concatenate([(jax.nn.silu(gate[e]) * up[e] * ce[e]).astype(BF16) for e in experts], axis=1)
        ys[pl.ds(r0, win), :] += _dot(act, wd_ref[...])
        return carry

    lax.fori_loop(0, n_win, window, 0)

    @pl.when(g == pl.num_programs(1) - 1)
    def _unsort():
        def body(r, carry):
            d = dest_ref[base + r]
            hf[pl.ds(r, 1), :] = ys[pl.ds(d, 1), :]
            return carry

        lax.fori_loop(0, tm, body, 0, unroll=8)
        o_ref[...] = x_ref[...] + hf[...]


def _moe(x, h, comb, w_g, w_u, w_d, tm, layer):
    n, d = x.shape
    win = tm // N_GROUPS + 48
    rows = tm + N_GROUPS * SUBLANES + win
    dest, gstart, counts = _dispatch_plan(comb, tm)
    per_group = pl.BlockSpec((None, EXPERTS_PER_GROUP) + w_g.shape[2:], lambda i, g, *_: (layer, g, 0, 0))
    row = lambda i, g, *_: (i, 0)
    grid_spec = pltpu.PrefetchScalarGridSpec(
        num_scalar_prefetch=3,
        grid=(n // tm, N_GROUPS),
        in_specs=[pl.BlockSpec((tm, d), row), pl.BlockSpec((tm, d), row), pl.BlockSpec((tm, LANES), row),
                  per_group, per_group,
                  pl.BlockSpec((None, None) + w_d.shape[2:], lambda i, g, *_: (layer, g, 0, 0))],
        out_specs=pl.BlockSpec((tm, d), row),
        scratch_shapes=[pltpu.VMEM((tm, d), F32), pltpu.VMEM((rows, d), F32),
                        pltpu.VMEM((rows, LANES), F32), pltpu.VMEM((rows, d), F32)],
    )
    return pl.pallas_call(
        functools.partial(_moe_kernel, tm=tm, win=win),
        grid_spec=grid_spec,
        out_shape=jax.ShapeDtypeStruct((n, d), F32),
        compiler_params=_params(("parallel", "arbitrary")),
        name="moe",
    )(dest, gstart, counts, x, h, comb, w_g, w_u, w_d)


def _rope_tables(positions):
    half = ROT_DIM // 2
    inv = ROPE_THETA ** (-jnp.arange(0, ROT_DIM, 2, dtype=F32) / ROT_DIM)
    ang = positions.astype(F32)[:, None] * inv[None, :]
    cos, sin = jnp.cos(ang), jnp.sin(ang)
    n = positions.shape[0]
    rest = HEAD_DIM - ROT_DIM
    comp_c = jnp.concatenate([cos, cos, jnp.ones((n, rest), F32)], axis=1)
    comp_a = jnp.concatenate([-sin, jnp.zeros((n, half + rest), F32)], axis=1)
    comp_b = jnp.concatenate([jnp.zeros((n, half), F32), sin, jnp.zeros((n, rest), F32)], axis=1)
    return tuple(jnp.tile(c, (1, LANES // HEAD_DIM)) for c in (comp_c, comp_a, comp_b))


def _block_diag(blocks):
    g, a, b = blocks.shape
    tiled = jnp.tile(blocks.reshape(g * a, b), (1, g))
    rg = lax.broadcasted_iota(jnp.int32, (g * a, g * b), 0) // a
    cg = lax.broadcasted_iota(jnp.int32, (g * a, g * b), 1) // b
    return jnp.where(rg == cg, tiled, jnp.zeros_like(tiled))


def kernel(x_prompt, x_sample, cache_kb, cache_vb, cache_kc, cache_vc, state_pool, page_table, norm1_g, w_in, pool_w, pool_scale, qn_g, kn_g, lam_q1, lam_k1, lam_q2, lam_k2, subln_g, w_proj_a, w_proj_b, w_proj_c, w_out, norm2_g, w_group, b_group, w_router, b_router, w_e_gate, w_e_up, w_e_down):
    bp, tp, d = x_prompt.shape
    bs, ts, _ = x_sample.shape
    depth = w_in.shape[0]
    n_pages, page = page_table.shape[1], cache_kb.shape[2]
    past_len = n_pages * page
    pool_width = pool_w.shape[1] * pool_w.shape[2]
    diff_heads, stick_heads = cache_kb.shape[3], cache_kc.shape[3]
    diff_w = diff_heads * 2 * HEAD_DIM
    stick_w = stick_heads * HEAD_DIM
    sizes = (pool_width, diff_w, diff_w, diff_w, stick_w, stick_w, stick_w, d, d, d)

    tm_p = min(512, tp)
    tm_s = min(512, bs * ts)
    tq = min(256, tp)
    tk = min(512, tp)
    tm_moe_p = min(1024, bp * tp)
    tm_moe_s = min(1024, bs * ts)
    tabs_p = _rope_tables(jnp.arange(tp))
    tabs_s = _rope_tables(past_len + (jnp.arange(tm_s) % ts))
    pos_block_p = lambda i: i % (tp // tm_p)
    pos_block_s = lambda i: 0

    gm = _block_diag(jnp.full((diff_w // HEAD_DIM, HEAD_DIM, HEAD_DIM), 1.0 / HEAD_DIM, F32)).astype(BF16)
    n_pool_pages = cache_kb.shape[1]
    ckb, cvb = (c.reshape(depth, n_pool_pages, page * diff_heads, 2 * HEAD_DIM) for c in (cache_kb, cache_vb))
    ckc, cvc = (c.transpose(0, 1, 3, 4, 2).reshape(depth, n_pool_pages, stick_w, page) for c in (cache_kc, cache_vc))
    zero_pool = jnp.zeros((bp, POOL_STATE, pool_width), F32)

    xp = x_prompt.reshape(bp * tp, d)
    xs = x_sample.reshape(bs * ts, d)
    pool_rows_p, rows_s = [], []
    cache_p = ()
    w_in_b, wa, wb, wc, wo, w_g, w_u = (w.astype(BF16) for w in (w_in, w_proj_a, w_proj_b, w_proj_c, w_out,
                                                                 w_e_gate, w_e_up))
    w_d = w_e_down.reshape(depth, N_GROUPS, -1, d).astype(BF16)
    for l in range(depth):
        lam_init = 0.8 - 0.6 * math.exp(-0.3 * l)
        g1 = norm1_g[l].reshape(1, d)
        qg = jnp.tile(qn_g[l].reshape(1, -1), (1, diff_heads))
        kg = jnp.tile(kn_g[l].reshape(1, -1), (1, diff_heads))
        w_bd = _block_diag(pool_w[l]).astype(BF16)
        p_scale = pool_scale[l].reshape(1, -1)
        lam_vecs = tuple(v[l].reshape(1, -1) for v in (lam_q1, lam_k1, lam_q2, lam_k2))
        sg = subln_g[l].reshape(1, -1)
        g2 = norm2_g[l].reshape(1, d)
        wr = jnp.concatenate([w_group[l], w_router[l],
                              jnp.zeros((d, LANES - N_GROUPS - N_EXPERTS), F32)], axis=1)
        wr_hi = wr.astype(BF16)
        wr_lo = (wr - wr_hi.astype(F32)).astype(BF16)
        br = jnp.concatenate([b_group[l], b_router[l],
                              jnp.zeros((LANES - N_GROUPS - N_EXPERTS,), F32)]).reshape(1, LANES)

        def tail(x, o_a, o_b, o_c, gates, tm, tm_moe):
            x_mid, h2, comb = _merge(x, o_a, o_b, o_c, gates, wa, wb, wc, wo, g2, wr_hi, wr_lo, br, tm, l)
            return _moe(x_mid, h2, comb, w_g, w_u, w_d, tm_moe, l)

        a_in, qb, kb, vb, qc, kc, vc, gates = _in_proj(
            xp, g1, w_in_b, qg, kg, gm, tabs_p, pos_block_p, sizes, tm_p, tp, True, l, depth, cache_p)
        cache_p = (kb, vb, kc, vc)
        o_a = _pool(a_in, zero_pool, w_bd, p_scale, bp, tp, 0)
        o_b = _diff_attn_prompt(qb, kb, vb, lam_vecs, sg, bp, tp, lam_init, tq, tk, 4, l)
        o_c = _stick_prompt(qc, kc, vc, bp, tp, tq, l)
        xp = tail(xp, o_a, o_b, o_c, gates, tm_p, tm_moe_p)
        pool_rows_p.append(a_in.reshape(bp, tp, -1)[:, tp - POOL_STATE:])

        a_in, qb, kb, vb, qc, kc, vc, gates = _in_proj(
            xs, g1, w_in_b, qg, kg, gm, tabs_s, pos_block_s, sizes, tm_s, ts, False, l)
        o_a = _pool(a_in, state_pool[l], w_bd, p_scale, bs, ts, past_len)
        o_b = _diff_decode(qb, kb, vb, ckb, cvb, page_table, l, lam_vecs, sg, ts, lam_init)
        o_c = _stick_decode(qc, kc, vc, ckc, cvc, page_table, l, ts)
        xs = tail(xs, o_a, o_b, o_c, gates, tm_s, tm_moe_s)
        new_pool = jnp.concatenate([state_pool[l], a_in.reshape(bs, ts, -1)], axis=1)[:, -POOL_STATE:]
        rows_s.append((kb, vb, kc, vc, new_pool))

    kb_p, vb_p, kc_p, vc_p = cache_p
    pool_p = jnp.stack(pool_rows_p)
    kb_s, vb_s, kc_s, vc_s, pool_s = [jnp.stack(z) for z in zip(*rows_s)]
    diff_shape = lambda b, t: (depth, b, t, diff_heads, 2 * HEAD_DIM)
    stick_t = lambda z: z.reshape(depth, bp, stick_heads, HEAD_DIM, tp).transpose(0, 1, 4, 2, 3)
    return (xp.reshape(bp, tp, d), xs.reshape(bs, ts, d),
            kb_p.reshape(diff_shape(bp, tp)), vb_p.reshape(diff_shape(bp, tp)), stick_t(kc_p), stick_t(vc_p), pool_p,
            kb_s.reshape(diff_shape(bs, ts)), vb_s.reshape(diff_shape(bs, ts)),
            kc_s.reshape(depth, bs, ts, stick_heads, HEAD_DIM), vc_s.reshape(depth, bs, ts, stick_heads, HEAD_DIM),
            pool_s)
```
